```python
import math
import jax, jax.numpy as jnp
from jax import lax
import numpy as np

D_MODEL = 1024
BATCH = 2
SEQ = 8192
DEPTH = 4

GRID_W = 64
CTX_LEN = 256
NORM_EPS = 1e-6
ROPE_BASE = 10000.0
DA_HEADS = 4
DA_HEAD_DIM = 64
DA_V_DIM = 2 * DA_HEAD_DIM
DA_WIDTH = DA_HEADS * DA_V_DIM
ATTN_BLOCK = 128
RET_HEADS = 4
RET_QK_DIM = 64
RET_V_DIM = 64
RET_WIDTH = RET_HEADS * RET_V_DIM
RET_CHUNK = 128
SG_GROUPS = 4
SG_GROUP_DIM = 64
SG_WIDTH = SG_GROUPS * SG_GROUP_DIM
SG_CHUNK = 128
MIX_WIDTH = DA_WIDTH + RET_WIDTH + SG_WIDTH
IN_SPLITS = (DA_HEADS * 2 * DA_HEAD_DIM, DA_HEADS * 2 * DA_HEAD_DIM, DA_WIDTH,
             RET_HEADS * RET_QK_DIM, RET_HEADS * RET_QK_DIM, RET_WIDTH, RET_WIDTH, RET_WIDTH,
             SG_WIDTH, SG_WIDTH)
IN_COLS = 3 * DA_WIDTH + 5 * RET_WIDTH + 2 * SG_WIDTH
N_EXPERTS = 16
EXPERT_FF = 2 * D_MODEL
EC_CAPACITY = 2

kernel_name = "hybrid_diffattn_retention_sgmlp_ec_moe_dit"


def rms_norm(x, g):
    xf = x.astype(jnp.float32)
    y = xf * lax.rsqrt(jnp.mean(xf * xf, axis=-1, keepdims=True) + NORM_EPS)
    return (y * g.astype(jnp.float32)).astype(x.dtype)


def layer_norm_f32(x):
    xf = x.astype(jnp.float32)
    mu = jnp.mean(xf, axis=-1, keepdims=True)
    var = jnp.mean(jnp.square(xf - mu), axis=-1, keepdims=True)
    return (xf - mu) * lax.rsqrt(var + NORM_EPS)


def split_cols(z):
    out, off = [], 0
    for w in IN_SPLITS:
        out.append(z[..., off:off + w])
        off += w
    return out


def split_heads(t, n_heads):
    b, n, _ = t.shape
    return t.reshape(b, n, n_heads, -1).transpose(0, 2, 1, 3)


def merge_heads(t):
    b, h, n, d = t.shape
    return t.transpose(0, 2, 1, 3).reshape(b, n, h * d)


def flip_seq(t):
    return t[:, :, ::-1]


def rotate_pairs(x, ang):
    m = ang.shape[-1]
    cos = jnp.cos(ang).astype(x.dtype)
    sin = jnp.sin(ang).astype(x.dtype)
    x1, x2 = x[..., :m], x[..., m:]
    return jnp.concatenate([x1 * cos - x2 * sin, x1 * sin + x2 * cos], axis=-1)


def axial_angles(n):
    rows = n // GRID_W
    row = jnp.broadcast_to(jnp.arange(rows, dtype=jnp.float32)[:, None], (rows, GRID_W)).reshape(-1)
    col = jnp.broadcast_to(jnp.arange(GRID_W, dtype=jnp.float32)[None, :], (rows, GRID_W)).reshape(-1)
    quarter = DA_HEAD_DIM // 4
    inv = ROPE_BASE ** (-jnp.arange(quarter, dtype=jnp.float32) / quarter)
    return row[:, None] * inv, col[:, None] * inv


def rope_2d(x, ang_r, ang_c):
    half = DA_HEAD_DIM // 2
    return jnp.concatenate([rotate_pairs(x[..., :half], ang_r), rotate_pairs(x[..., half:], ang_c)], axis=-1)


def retention_angles(n):
    inv = 1.0 / (ROPE_BASE ** jnp.linspace(0.0, 1.0, RET_QK_DIM // 2, dtype=jnp.float32))
    return jnp.arange(n, dtype=jnp.float32)[:, None] * inv


def da_split_norm(t, g):
    th = split_heads(t, DA_HEADS)
    return rms_norm(th[..., :DA_HEAD_DIM], g), rms_norm(th[..., DA_HEAD_DIM:], g)


def diff_attention(q1, q2, k1, k2, v, lam):
    scale = DA_HEAD_DIM ** -0.5
    s1 = jnp.einsum('bhqd,bhkd->bhqk', q1, k1).astype(jnp.float32) * scale
    s2 = jnp.einsum('bhqd,bhkd->bhqk', q2, k2).astype(jnp.float32) * scale
    a = jax.nn.softmax(s1, axis=-1) - lam * jax.nn.softmax(s2, axis=-1)
    return jnp.einsum('bhqk,bhkv->bhqv', a.astype(v.dtype), v)


def diff_attention_blocked(q1, q2, k1, k2, v, lam):
    b, h, n, dh = q1.shape
    nb = n // ATTN_BLOCK

    def to_blocks(q):
        return jnp.moveaxis(q.reshape(b, h, nb, ATTN_BLOCK, dh), 2, 0)

    out = lax.map(lambda qs: diff_attention(qs[0], qs[1], k1, k2, v, lam), (to_blocks(q1), to_blocks(q2)))
    return jnp.moveaxis(out, 0, 2).reshape(b, h, n, v.shape[-1])


def da_output(o, subln_g, lambda_init):
    return merge_heads(rms_norm(o, subln_g) * (1.0 - lambda_init))


def retention_chunkwise(q, k, v, log_gamma, s0):
    b, h, n, dk = q.shape
    dv = v.shape[-1]
    L = RET_CHUNK
    nc = n // L
    pos = jnp.arange(L, dtype=jnp.float32)
    lg = log_gamma.astype(jnp.float32)[:, None, None]
    dist = pos[:, None] - pos[None, :]
    intra = jnp.where(dist >= 0, jnp.exp(lg * jnp.maximum(dist, 0.0)), 0.0)
    q_decay = jnp.exp(lg[:, :, 0] * (pos + 1.0))[:, :, None]
    k_decay = jnp.exp(lg[:, :, 0] * (L - 1.0 - pos))[:, :, None]
    chunk_decay = jnp.exp(lg * L)

    def chunks(t):
        return jnp.moveaxis(t.astype(jnp.float32).reshape(b, h, nc, L, t.shape[-1]), 2, 0)

    def step(s, qkv):
        qc, kc, vc = qkv
        scores = jnp.einsum('bhid,bhjd->bhij', qc, kc) * intra
        o = jnp.einsum('bhij,bhjv->bhiv', scores, vc) + jnp.einsum('bhid,bhdv->bhiv', qc, s) * q_decay
        s = s * chunk_decay + jnp.einsum('bhjd,bhjv->bhdv', kc * k_decay, vc)
        return s, o

    s, o = lax.scan(step, s0.astype(jnp.float32), (chunks(q), chunks(k), chunks(v)))
    return jnp.moveaxis(o, 0, 2).reshape(b, h, n, dv), s


def ret_heads(rq, rk, rv, ang):
    q = split_heads(rq, RET_HEADS)
    k = split_heads(rk, RET_HEADS) * (RET_QK_DIM ** -0.5)
    v = split_heads(rv, RET_HEADS)
    if ang is not None:
        q = rotate_pairs(q, ang)
        k = rotate_pairs(k, ang)
    return q, k, v


def retention_output(o_f, o_b, g_f, g_b):
    y_f = merge_heads(layer_norm_f32(o_f)).astype(g_f.dtype) * jax.nn.silu(g_f)
    y_b = merge_heads(layer_norm_f32(o_b)).astype(g_b.dtype) * jax.nn.silu(g_b)
    return y_f + y_b


def spatial_gating(su, sv, norm_g, w_s, b_s):
    u = jax.nn.gelu(su)
    v = (layer_norm_f32(jax.nn.gelu(sv)) * norm_g.astype(jnp.float32)).astype(su.dtype)
    b, n, _ = v.shape
    vb = v.reshape(b, n // SG_CHUNK, SG_CHUNK, SG_GROUPS, SG_GROUP_DIM)
    s = jnp.einsum('gij,bcjgd->bcigd', w_s, vb) + jnp.swapaxes(b_s, 0, 1)[:, :, None]
    return u * s.reshape(b, n, SG_WIDTH)


def token_mixer(h_lat, h_ctx, w_in, w_out, q_norm_g, k_norm_g, lam, lambda_init, subln_g,
                log_gamma, sg_norm_g, sg_w, sg_b, ang_r, ang_c, ret_ang, need_ctx):
    aq, ak, av, rq, rk, rv, rgf, rgb, su, sv = split_cols(h_lat @ w_in)
    caq, cak, cav, crq, crk, crv, crgf, crgb, csu, csv = split_cols(h_ctx @ w_in)

    q1, q2 = da_split_norm(aq, q_norm_g)
    k1, k2 = da_split_norm(ak, k_norm_g)
    q1, q2, k1, k2 = (rope_2d(t, ang_r, ang_c) for t in (q1, q2, k1, k2))
    cq1, cq2 = da_split_norm(caq, q_norm_g)
    ck1, ck2 = da_split_norm(cak, k_norm_g)
    v = split_heads(av, DA_HEADS)
    cv = split_heads(cav, DA_HEADS)
    k1_all = jnp.concatenate([ck1, k1], axis=2)
    k2_all = jnp.concatenate([ck2, k2], axis=2)
    v_all = jnp.concatenate([cv, v], axis=2)
    a_lat = da_output(diff_attention_blocked(q1, q2, k1_all, k2_all, v_all, lam), subln_g, lambda_init)

    cq, ck, cvr = ret_heads(crq, crk, crv, None)
    b = h_lat.shape[0]
    zero = jnp.zeros((b, RET_HEADS, RET_QK_DIM, RET_V_DIM), jnp.float32)
    co_f, cs_f = retention_chunkwise(cq, ck, cvr, log_gamma[0], zero)
    co_b, cs_b = retention_chunkwise(flip_seq(cq), flip_seq(ck), flip_seq(cvr), log_gamma[1], zero)
    rq_h, rk_h, rv_h = ret_heads(rq, rk, rv, ret_ang)
    o_f, _ = retention_chunkwise(rq_h, rk_h, rv_h, log_gamma[0], cs_f)
    o_b, _ = retention_chunkwise(flip_seq(rq_h), flip_seq(rk_h), flip_seq(rv_h), log_gamma[1], cs_b)
    r_lat = retention_output(o_f, flip_seq(o_b), rgf, rgb)

    s_lat = spatial_gating(su, sv, sg_norm_g, sg_w, sg_b)

    y_lat = jnp.concatenate([a_lat, r_lat, s_lat], axis=-1) @ w_out
    if not need_ctx:
        return y_lat, None
    a_ctx = da_output(diff_attention(cq1, cq2, ck1, ck2, cv, lam), subln_g, lambda_init)
    r_ctx = retention_output(co_f, flip_seq(co_b), crgf, crgb)
    s_ctx = spatial_gating(csu, csv, sg_norm_g, sg_w, sg_b)
    y_ctx = jnp.concatenate([a_ctx, r_ctx, s_ctx], axis=-1) @ w_out
    return y_lat, y_ctx


def expert_choice_ffn(h, router_w, w_gate, w_up, w_down):
    b, n, d = h.shape
    cap = EC_CAPACITY * n // N_EXPERTS
    aff = jax.nn.softmax((h @ router_w).astype(jnp.float32), axis=-1)
    g, idx = lax.top_k(jnp.swapaxes(aff, 1, 2), cap)
    xs = jax.vmap(lambda hb, ib: hb[ib])(h, idx)
    a = jnp.einsum('becd,edf->becf', xs, w_gate)
    u = jnp.einsum('becd,edf->becf', xs, w_up)
    y = jnp.einsum('becf,efd->becd', jax.nn.silu(a) * u, w_down) * g[..., None].astype(h.dtype)
    return jax.vmap(lambda yb, ib: jnp.zeros((n, d), h.dtype).at[ib.reshape(-1)].add(yb.reshape(-1, d)))(y, idx)


def setup_inputs(seed: int = 0) -> dict:
    key = jax.random.key(seed)
    ks = jax.random.split(key, 26)
    D = D_MODEL

    def nrm(k, shape, scale):
        return jax.random.normal(k, shape, jnp.float32) * scale

    heads = jnp.arange(RET_HEADS, dtype=jnp.float32)
    ret_init = jnp.log(2.0 ** (5.0 + heads) - 1.0)
    return {
        'x': nrm(ks[0], (BATCH, SEQ, D), 1.0),
        'c': nrm(ks[1], (BATCH, D), 1.0),
        'ctx': nrm(ks[2], (BATCH, CTX_LEN, D), 1.0),
        'c_ctx': nrm(ks[3], (D,), 1.0),
        'w_mod': nrm(ks[4], (DEPTH, D, 6 * D), 0.5 * D ** -0.5),
        'b_mod': nrm(ks[5], (DEPTH, 6 * D), 0.02),
        'norm1_g': 1.0 + nrm(ks[6], (DEPTH, D), 0.05),
        'norm2_g': 1.0 + nrm(ks[7], (DEPTH, D), 0.05),
        'w_in': nrm(ks[8], (DEPTH, D, IN_COLS), D ** -0.5),
        'w_out': nrm(ks[9], (DEPTH, MIX_WIDTH, D), MIX_WIDTH ** -0.5),
        'da_q_norm_g': 1.0 + nrm(ks[10], (DEPTH, DA_HEAD_DIM), 0.05),
        'da_k_norm_g': 1.0 + nrm(ks[11], (DEPTH, DA_HEAD_DIM), 0.05),
        'da_lambda_q1': nrm(ks[12], (DEPTH, DA_HEAD_DIM), 0.1),
        'da_lambda_k1': nrm(ks[13], (DEPTH, DA_HEAD_DIM), 0.1),
        'da_lambda_q2': nrm(ks[14], (DEPTH, DA_HEAD_DIM), 0.1),
        'da_lambda_k2': nrm(ks[15], (DEPTH, DA_HEAD_DIM), 0.1),
        'da_subln_g': 1.0 + nrm(ks[16], (DEPTH, DA_V_DIM), 0.05),
        'ret_decay': ret_init + nrm(ks[17], (DEPTH, 2, RET_HEADS), 0.1),
        'sg_norm_g': 1.0 + nrm(ks[18], (DEPTH, SG_WIDTH), 0.05),
        'sg_w': nrm(ks[19], (DEPTH, SG_GROUPS, SG_CHUNK, SG_CHUNK), SG_CHUNK ** -0.5),
        'sg_b': 1.0 + nrm(ks[20], (DEPTH, SG_GROUPS, SG_CHUNK), 0.1),
        'router_w': nrm(ks[21], (DEPTH, D, N_EXPERTS), D ** -0.5),
        'ex_w_gate': nrm(ks[22], (DEPTH, N_EXPERTS, D, EXPERT_FF), D ** -0.5),
        'ex_w_up': nrm(ks[23], (DEPTH, N_EXPERTS, D, EXPERT_FF), D ** -0.5),
        'ex_w_down': nrm(ks[24], (DEPTH, N_EXPERTS, EXPERT_FF, D), EXPERT_FF ** -0.5),
    }


def reference(x, c, ctx, c_ctx, w_mod, b_mod, norm1_g, norm2_g, w_in, w_out, da_q_norm_g, da_k_norm_g,
              da_lambda_q1, da_lambda_k1, da_lambda_q2, da_lambda_k2, da_subln_g, ret_decay,
              sg_norm_g, sg_w, sg_b, router_w, ex_w_gate, ex_w_up, ex_w_down):
    n = x.shape[1]
    ang_r, ang_c = axial_angles(n)
    ret_ang = retention_angles(n)
    for layer in range(DEPTH):
        need_ctx = layer < DEPTH - 1
        lambda_init = 0.8 - 0.6 * math.exp(-0.3 * layer)
        mod = jax.nn.silu(c) @ w_mod[layer] + b_mod[layer]
        mod_c = jax.nn.silu(c_ctx) @ w_mod[layer] + b_mod[layer]
        sh1, sc1, g1, sh2, sc2, g2 = jnp.split(mod[:, None, :], 6, axis=-1)
        csh1, csc1, cg1, csh2, csc2, cg2 = jnp.split(mod_c, 6, axis=-1)
        lam = (jnp.exp(jnp.sum(da_lambda_q1[layer].astype(jnp.float32) * da_lambda_k1[layer].astype(jnp.float32)))
               - jnp.exp(jnp.sum(da_lambda_q2[layer].astype(jnp.float32) * da_lambda_k2[layer].astype(jnp.float32)))
               + lambda_init)
        log_gamma = jax.nn.log_sigmoid(ret_decay[layer].astype(jnp.float32))

        h_lat = rms_norm(x, norm1_g[layer]) * (1.0 + sc1) + sh1
        h_ctx = rms_norm(ctx, norm1_g[layer]) * (1.0 + csc1) + csh1
        y_lat, y_ctx = token_mixer(h_lat, h_ctx, w_in[layer], w_out[layer], da_q_norm_g[layer], da_k_norm_g[layer],
                                   lam, lambda_init, da_subln_g[layer], log_gamma, sg_norm_g[layer], sg_w[layer],
                                   sg_b[layer], ang_r, ang_c, ret_ang, need_ctx)
        x = x + g1 * y_lat
        f_lat = rms_norm(x, norm2_g[layer]) * (1.0 + sc2) + sh2
        x = x + g2 * expert_choice_ffn(f_lat, router_w[layer], ex_w_gate[layer], ex_w_up[layer], ex_w_down[layer])
        if need_ctx:
            ctx = ctx + cg1 * y_ctx
            f_ctx = rms_norm(ctx, norm2_g[layer]) * (1.0 + csc2) + csh2
            ctx = ctx + cg2 * expert_choice_ffn(f_ctx, router_w[layer], ex_w_gate[layer], ex_w_up[layer], ex_w_down[layer])
    return x
```

```python
import functools
import math

import jax
import jax.numpy as jnp
from jax import lax
from jax.experimental import pallas as pl
from jax.experimental.pallas import tpu as pltpu

F32 = jnp.float32
BF16 = jnp.bfloat16

D_MODEL = 1024
DEPTH = 4
GRID_W = 64
NORM_EPS = 1e-6
ROPE_BASE = 10000.0
DA_HEADS = 4
DA_HEAD_DIM = 64
RET_HEADS = 4
RET_QK_DIM = 64
RET_CHUNK = 128
SG_GROUPS = 4
SG_GROUP_DIM = 64
SG_CHUNK = 128
N_EXPERTS = 16
EXPERT_FF = 2 * D_MODEL
EC_CAPACITY = 2
IN_COLS = 3328
COL_AQ, COL_AK, COL_AV = 0, 4, 8
COL_RQ, COL_RK, COL_RV, COL_RGF, COL_RGB = 12, 14, 16, 18, 20
COL_SU256, COL_SV256 = 11, 12

V7X_LANES = 128
V7X_VMEM_BYTES = 64 * 1024 * 1024
NEG_BIG = -1e30


def _cparams(semantics, vmem_mib):
    assert vmem_mib * 1024 * 1024 < V7X_VMEM_BYTES
    return pltpu.CompilerParams(dimension_semantics=semantics, vmem_limit_bytes=vmem_mib * 1024 * 1024)


def _dot(a, b):
    return jnp.dot(a, b, preferred_element_type=F32)


def _dot_nt(a, b):
    return lax.dot_general(a, b, (((1,), (1,)), ((), ())), preferred_element_type=F32)


def _dot_tn(a, b):
    return lax.dot_general(a, b, (((0,), (0,)), ((), ())), preferred_element_type=F32)


def _seg_dot(x, seg):
    hi = x.astype(BF16)
    lo = (x - hi.astype(F32)).astype(BF16)
    return _dot(hi, seg) + _dot(lo, seg)


def _sigmoid(x):
    return 1.0 / (1.0 + jnp.exp(-x))


def _in_proj_body(x_ref, a_ref, s_ref, w_ref, z_ref):
    x = x_ref[...]
    ms = jnp.mean(x * x, axis=-1, keepdims=True)
    h = x * lax.rsqrt(ms + NORM_EPS) * a_ref[...] + s_ref[...]
    z_ref[...] = _dot(h.astype(BF16), w_ref[...])


def in_proj(x, a, s, w, tm):
    b, r, d = x.shape
    c = w.shape[1]
    return pl.pallas_call(
        _in_proj_body,
        grid=(b, r // tm),
        in_specs=[
            pl.BlockSpec((None, tm, d), lambda i, j: (i, j, 0)),
            pl.BlockSpec((None, 1, d), lambda i, j: (i, 0, 0)),
            pl.BlockSpec((None, 1, d), lambda i, j: (i, 0, 0)),
            pl.BlockSpec((d, c), lambda i, j: (0, 0)),
        ],
        out_specs=pl.BlockSpec((None, tm, c), lambda i, j: (i, j, 0)),
        out_shape=jax.ShapeDtypeStruct((b, r, c), F32),
        compiler_params=_cparams(("parallel", "parallel"), 48),
        name="in_proj",
    )(x, a, s, w)


def _qkv_prep_body(q_ref, k_ref, v_ref, cos_ref, sin_ref, qg_ref, kg_ref, seg_ref,
                   qp1_ref, qp2_ref, kk_ref, vt_ref):
    lane = lax.broadcasted_iota(jnp.int32, (1, V7X_LANES), 1)
    lower16 = (lane % 32) < 16
    seg = seg_ref[...]
    cos = cos_ref[...]
    sin = sin_ref[...]

    def norm_rope(x, g):
        ms = _seg_dot(x * x, seg) * (1.0 / DA_HEAD_DIM)
        y = x * lax.rsqrt(ms + NORM_EPS) * g
        partner = jnp.where(lower16, pltpu.roll(y, 112, 1), pltpu.roll(y, 16, 1))
        return y * cos + partner * sin

    q = norm_rope(q_ref[...], qg_ref[...]) * (DA_HEAD_DIM ** -0.5)
    first = lane < DA_HEAD_DIM
    qp1_ref[...] = jnp.where(first, q, 0.0).astype(BF16)
    qp2_ref[...] = jnp.where(first, 0.0, q).astype(BF16)
    kk_ref[...] = norm_rope(k_ref[...], kg_ref[...]).astype(BF16)
    vt_ref[...] = v_ref[...].T.astype(BF16)


def qkv_prep(z, cos, sin, qg, kg, seg, tm):
    b, r, _ = z.shape
    h = DA_HEADS
    row = lambda off: pl.BlockSpec((None, tm, V7X_LANES), lambda i, j, k: (i, k, off + j))
    tab = pl.BlockSpec((tm, V7X_LANES), lambda i, j, k: (k, 0))
    vec = pl.BlockSpec((1, V7X_LANES), lambda i, j, k: (0, 0))
    o4 = pl.BlockSpec((None, None, tm, V7X_LANES), lambda i, j, k: (i, j, k, 0))
    sd = jax.ShapeDtypeStruct((b, h, r, V7X_LANES), BF16)
    return pl.pallas_call(
        _qkv_prep_body,
        grid=(b, h, r // tm),
        in_specs=[row(COL_AQ), row(COL_AK), row(COL_AV), tab, tab, vec, vec,
                  pl.BlockSpec((V7X_LANES, V7X_LANES), lambda i, j, k: (0, 0))],
        out_specs=[o4, o4, o4,
                   pl.BlockSpec((None, None, None, V7X_LANES, tm), lambda i, j, k: (i, j, k, 0, 0))],
        out_shape=[sd, sd, sd, jax.ShapeDtypeStruct((b, h, r // tm, V7X_LANES, tm), BF16)],
        compiler_params=_cparams(("parallel", "parallel", "parallel"), 32),
        name="qkv_prep",
    )(z, z, z, cos, sin, qg, kg, seg)


def _attn_body(*refs, n_lat, tk):
    if n_lat:
        lam_ref, q1_ref, q2_ref, kc_ref, vc_ref, kl_ref, vl_ref, g_ref, o_ref = refs
    else:
        lam_ref, q1_ref, q2_ref, kc_ref, vc_ref, g_ref, o_ref = refs
    q1 = q1_ref[...]
    q2 = q2_ref[...]
    tq = q1.shape[0]

    def update(qp, k, vt, m, l, acc):
        s = _dot_nt(k, qp)
        mn = jnp.maximum(m, jnp.max(s, axis=0, keepdims=True))
        alpha = jnp.exp(m - mn)
        p = jnp.exp(s - mn)
        l = alpha * l + jnp.sum(p, axis=0, keepdims=True)
        acc = alpha * acc + _dot(vt, p.astype(BF16))
        return mn, l, acc

    def step(k, vt, carry):
        m1, l1, a1, m2, l2, a2 = carry
        m1, l1, a1 = update(q1, k, vt, m1, l1, a1)
        m2, l2, a2 = update(q2, k, vt, m2, l2, a2)
        return m1, l1, a1, m2, l2, a2

    row = lambda v: jnp.full((1, tq), v, F32)
    zacc = jnp.zeros((V7X_LANES, tq), F32)
    carry = (row(NEG_BIG), row(0.0), zacc, row(NEG_BIG), row(0.0), zacc)
    carry = step(kc_ref[...], vc_ref[0], carry)
    if n_lat:
        def body(i, c):
            start = pl.multiple_of(i * tk, tk)
            return step(kl_ref[pl.ds(start, tk), :], vl_ref[i], c)
        carry = lax.fori_loop(0, n_lat, body, carry)
    m1, l1, a1, m2, l2, a2 = carry
    o = a1 / l1 - lam_ref[0] * (a2 / l2)
    ms = jnp.mean(o * o, axis=0, keepdims=True)
    y = o * lax.rsqrt(ms + NORM_EPS) * g_ref[...]
    o_ref[...] = y.T.astype(BF16)


def diff_attention(lam, q1, q2, kc, vc, kl, vl, gcol, tq):
    b, h, nq, _ = q1.shape
    nc = kc.shape[2]
    qs = pl.BlockSpec((None, None, tq, V7X_LANES), lambda i, j, k: (i, j, k, 0))
    in_specs = [pl.BlockSpec(memory_space=pltpu.SMEM), qs, qs,
                pl.BlockSpec((None, None, nc, V7X_LANES), lambda i, j, k: (i, j, 0, 0)),
                pl.BlockSpec((None, None, 1, V7X_LANES, nc), lambda i, j, k: (i, j, 0, 0, 0))]
    args = [lam, q1, q2, kc, vc]
    n_lat, tk = 0, 0
    if kl is not None:
        n_lat, tk = vl.shape[2], vl.shape[4]
        in_specs += [pl.BlockSpec((None, None, kl.shape[2], V7X_LANES), lambda i, j, k: (i, j, 0, 0)),
                     pl.BlockSpec((None, None, n_lat, V7X_LANES, tk), lambda i, j, k: (i, j, 0, 0, 0))]
        args += [kl, vl]
    in_specs.append(pl.BlockSpec((V7X_LANES, 1), lambda i, j, k: (0, 0)))
    args.append(gcol)
    return pl.pallas_call(
        functools.partial(_attn_body, n_lat=n_lat, tk=tk),
        grid=(b, h, nq // tq),
        in_specs=in_specs,
        out_specs=pl.BlockSpec((None, tq, V7X_LANES), lambda i, j, k: (i, k, j)),
        out_shape=jax.ShapeDtypeStruct((b, nq, h * V7X_LANES), BF16),
        compiler_params=_cparams(("parallel", "parallel", "parallel"), 48),
        name="diff_attention",
    )(*args)


def _ret_body(qf_ref, kf_ref, vf_ref, gf_ref, cosf_ref, sinf_ref,
              qb_ref, kb_ref, vb_ref, gb_ref, cosb_ref, sinb_ref,
              dec_ref, intra_ref, seg_ref, s0_ref,
              yf_ref, yb_ref, sout_ref, state_ref, *, n_chunks):
    step = pl.program_id(2)
    lc = RET_CHUNK

    @pl.when(step == 0)
    def _():
        state_ref[...] = s0_ref[...]

    lane = lax.broadcasted_iota(jnp.int32, (1, V7X_LANES), 1)
    lower32 = (lane % RET_QK_DIM) < (RET_QK_DIM // 2)
    head0 = lane < RET_QK_DIM
    blk = (lax.broadcasted_iota(jnp.int32, (V7X_LANES, V7X_LANES), 0) // RET_QK_DIM ==
           lax.broadcasted_iota(jnp.int32, (V7X_LANES, V7X_LANES), 1) // RET_QK_DIM)
    seg = seg_ref[...]

    def rot(x, cos, sin):
        partner = jnp.where(lower32, pltpu.roll(x, 96, 1), pltpu.roll(x, 32, 1))
        return x * cos + partner * sin

    def chunk(d, q_ref, k_ref, v_ref, g_ref, cos_ref, sin_ref, y_ref, c):
        rows = pl.ds(c * lc, lc)
        cos = cos_ref[rows, :]
        sin = sin_ref[rows, :]
        q = rot(q_ref[rows, :], cos, sin)
        k = rot(k_ref[rows, :], cos, sin) * (RET_QK_DIM ** -0.5)
        v = v_ref[rows, :]
        qdec = dec_ref[d, 0]
        kdec = dec_ref[d, 1]
        cdec = dec_ref[d, 2][0:1, :]
        kb = k.astype(BF16)
        vb = v.astype(BF16)
        s = state_ref[d]
        o = _dot(q.astype(BF16), s.astype(BF16)) * qdec
        for hh in range(2):
            sel = head0 if hh == 0 else jnp.logical_not(head0)
            qm = jnp.where(sel, q, 0.0).astype(BF16)
            sc = _dot_nt(qm, kb) * intra_ref[d, hh]
            o = o + _dot(sc.astype(BF16), jnp.where(sel, v, 0.0).astype(BF16))
        kv = _dot_tn((k * kdec).astype(BF16), vb)
        state_ref[d] = s * cdec + jnp.where(blk, kv, 0.0)
        mu = _seg_dot(o, seg) * (1.0 / RET_QK_DIM)
        cen = o - mu
        var = _seg_dot(cen * cen, seg) * (1.0 / RET_QK_DIM)
        g = g_ref[rows, :]
        y_ref[rows, :] = cen * lax.rsqrt(var + NORM_EPS) * (g * _sigmoid(g))

    for c in range(n_chunks):
        chunk(0, qf_ref, kf_ref, vf_ref, gf_ref, cosf_ref, sinf_ref, yf_ref, c)
    for c in reversed(range(n_chunks)):
        chunk(1, qb_ref, kb_ref, vb_ref, gb_ref, cosb_ref, sinb_ref, yb_ref, c)

    @pl.when(step == pl.num_programs(2) - 1)
    def _():
        sout_ref[...] = state_ref[...]


def retention(z, cos, sin, dec, intra, seg, s0, tm):
    b, r, _ = z.shape
    nb = r // tm
    fwd = lambda off: pl.BlockSpec((None, tm, V7X_LANES), lambda i, p, k: (i, k, off + p))
    bwd = lambda off: pl.BlockSpec((None, tm, V7X_LANES), lambda i, p, k: (i, nb - 1 - k, off + p))
    tabf = pl.BlockSpec((tm, V7X_LANES), lambda i, p, k: (k, 0))
    tabb = pl.BlockSpec((tm, V7X_LANES), lambda i, p, k: (nb - 1 - k, 0))
    st = pl.BlockSpec((None, None, 2, V7X_LANES, V7X_LANES), lambda i, p, k: (i, p, 0, 0, 0))
    ysd = jax.ShapeDtypeStruct((b, r, 2 * V7X_LANES), F32)
    return pl.pallas_call(
        functools.partial(_ret_body, n_chunks=tm // RET_CHUNK),
        grid=(b, 2, nb),
        in_specs=[fwd(COL_RQ), fwd(COL_RK), fwd(COL_RV), fwd(COL_RGF), tabf, tabf,
                  bwd(COL_RQ), bwd(COL_RK), bwd(COL_RV), bwd(COL_RGB), tabb, tabb,
                  pl.BlockSpec((None, 2, 3, RET_CHUNK, V7X_LANES), lambda i, p, k: (p, 0, 0, 0, 0)),
                  pl.BlockSpec((None, 2, 2, RET_CHUNK, RET_CHUNK), lambda i, p, k: (p, 0, 0, 0, 0)),
                  pl.BlockSpec((V7X_LANES, V7X_LANES), lambda i, p, k: (0, 0)),
                  st],
        out_specs=[pl.BlockSpec((None, tm, V7X_LANES), lambda i, p, k: (i, k, p)),
                   pl.BlockSpec((None, tm, V7X_LANES), lambda i, p, k: (i, nb - 1 - k, p)),
                   st],
        out_shape=[ysd, ysd, jax.ShapeDtypeStruct(s0.shape, F32)],
        scratch_shapes=[pltpu.VMEM((2, V7X_LANES, V7X_LANES), F32)],
        compiler_params=_cparams(("parallel", "parallel", "arbitrary"), 32),
        name="retention",
    )(z, z, z, z, cos, sin, z, z, z, z, cos, sin, dec, intra, seg, s0)


def _sg_body(su_ref, sv_ref, ng_ref, w_ref, bm_ref, o_ref):
    tm = su_ref.shape[0]
    u = jax.nn.gelu(su_ref[...])
    t = jax.nn.gelu(sv_ref[...])
    mu = jnp.mean(t, axis=-1, keepdims=True)
    cen = t - mu
    var = jnp.mean(cen * cen, axis=-1, keepdims=True)
    v = (cen * lax.rsqrt(var + NORM_EPS) * ng_ref[...]).astype(BF16)
    group = lax.broadcasted_iota(jnp.int32, (1, 2 * V7X_LANES), 1) // SG_GROUP_DIM
    bias = bm_ref[...]
    for c in range(tm // SG_CHUNK):
        vc = v[c * SG_CHUNK:(c + 1) * SG_CHUNK, :]
        s = bias
        for g in range(SG_GROUPS):
            s = s + jnp.where(group == g, _dot(w_ref[g], vc), 0.0)
        o_ref[c * SG_CHUNK:(c + 1) * SG_CHUNK, :] = (u[c * SG_CHUNK:(c + 1) * SG_CHUNK, :] * s).astype(BF16)


def spatial_gating(z, ng, w, bm, tm):
    b, r, _ = z.shape
    wide = 2 * V7X_LANES
    return pl.pallas_call(
        _sg_body,
        grid=(b, r // tm),
        in_specs=[pl.BlockSpec((None, tm, wide), lambda i, j: (i, j, COL_SU256)),
                  pl.BlockSpec((None, tm, wide), lambda i, j: (i, j, COL_SV256)),
                  pl.BlockSpec((1, wide), lambda i, j: (0, 0)),
                  pl.BlockSpec((SG_GROUPS, SG_CHUNK, SG_CHUNK), lambda i, j: (0, 0, 0)),
                  pl.BlockSpec((SG_CHUNK, wide), lambda i, j: (0, 0))],
        out_specs=pl.BlockSpec((None, tm, wide), lambda i, j: (i, j, 0)),
        out_shape=jax.ShapeDtypeStruct((b, r, wide), BF16),
        compiler_params=_cparams(("parallel", "parallel"), 32),
        name="spatial_gating",
    )(z, z, ng, w, bm)


def _out_proj_body(a_ref, yf_ref, yb_ref, sg_ref, x_ref, g1_ref, a2_ref, s2_ref, w_ref, rwt_ref,
                   xo_ref, f_ref, aff_ref):
    na = a_ref.shape[1]
    nr = yf_ref.shape[1]
    r = (yf_ref[...] + yb_ref[...]).astype(BF16)
    y = (_dot(a_ref[...], w_ref[0:na, :]) + _dot(r, w_ref[na:na + nr, :])
         + _dot(sg_ref[...], w_ref[na + nr:, :]))
    xn = x_ref[...] + g1_ref[...] * y
    xo_ref[...] = xn
    ms = jnp.mean(xn * xn, axis=-1, keepdims=True)
    f = (xn * lax.rsqrt(ms + NORM_EPS) * a2_ref[...] + s2_ref[...]).astype(BF16)
    f_ref[...] = f
    logits = _dot_nt(rwt_ref[...], f)
    e = jnp.exp(logits - jnp.max(logits, axis=0, keepdims=True))
    aff_ref[...] = e / jnp.sum(e, axis=0, keepdims=True)


def out_proj(a, yf, yb, sg, x, g1, a2, s2, w, rwt, tm):
    b, r, d = x.shape
    rows = lambda width: pl.BlockSpec((None, tm, width), lambda i, j: (i, j, 0))
    vec = pl.BlockSpec((None, 1, d), lambda i, j: (i, 0, 0))
    return pl.pallas_call(
        _out_proj_body,
        grid=(b, r // tm),
        in_specs=[rows(a.shape[2]), rows(yf.shape[2]), rows(yb.shape[2]), rows(sg.shape[2]), rows(d),
                  vec, vec, vec,
                  pl.BlockSpec(w.shape, lambda i, j: (0, 0)),
                  pl.BlockSpec(rwt.shape, lambda i, j: (0, 0))],
        out_specs=[rows(d), rows(d), pl.BlockSpec((None, N_EXPERTS, tm), lambda i, j: (i, 0, j))],
        out_shape=[jax.ShapeDtypeStruct((b, r, d), F32), jax.ShapeDtypeStruct((b, r, d), BF16),
                   jax.ShapeDtypeStruct((b, N_EXPERTS, r), F32)],
        compiler_params=_cparams(("parallel", "parallel"), 32),
        name="out_proj",
    )(a, yf, yb, sg, x, g1, a2, s2, w, rwt)


def _ffn_body(x_ref, wg_ref, wu_ref, wd_ref, y_ref):
    c = pl.program_id(2)
    x = x_ref[...]
    a = _dot(x, wg_ref[...].astype(BF16))
    u = _dot(x, wu_ref[...].astype(BF16))
    h = (a * _sigmoid(a) * u).astype(BF16)
    part = _dot(h, wd_ref[...].astype(BF16))

    @pl.when(c == 0)
    def _():
        y_ref[...] = part

    @pl.when(c > 0)
    def _():
        y_ref[...] += part


def expert_ffn(xs, wg, wu, wd, tf):
    b, e, cap, d = xs.shape
    ff = wg.shape[2]
    return pl.pallas_call(
        _ffn_body,
        grid=(b, e, ff // tf),
        in_specs=[pl.BlockSpec((None, None, cap, d), lambda i, j, k: (i, j, 0, 0)),
                  pl.BlockSpec((None, d, tf), lambda i, j, k: (j, 0, k)),
                  pl.BlockSpec((None, d, tf), lambda i, j, k: (j, 0, k)),
                  pl.BlockSpec((None, tf, d), lambda i, j, k: (j, k, 0))],
        out_specs=pl.BlockSpec((None, None, cap, d), lambda i, j, k: (i, j, 0, 0)),
        out_shape=jax.ShapeDtypeStruct((b, e, cap, d), F32),
        compiler_params=_cparams(("parallel", "parallel", "arbitrary"), 48),
        name="expert_ffn",
    )(xs, wg, wu, wd)


def expert_choice(f, aff_t, wg, wu, wd):
    b, r, d = f.shape
    cap = EC_CAPACITY * r // N_EXPERTS
    g, idx = lax.top_k(aff_t, cap)
    xs = jax.vmap(lambda fb, ib: fb[ib])(f, idx)
    y = expert_ffn(xs, wg, wu, wd, 512) * g[..., None]
    return jax.vmap(lambda yb, ib: jnp.zeros((r, d), F32).at[ib.reshape(-1)].add(yb.reshape(-1, d)))(y, idx)


def _attn_tables(n):
    rows = n // GRID_W
    row = jnp.broadcast_to(jnp.arange(rows, dtype=F32)[:, None], (rows, GRID_W)).reshape(-1)
    col = jnp.broadcast_to(jnp.arange(GRID_W, dtype=F32)[None, :], (rows, GRID_W)).reshape(-1)
    quarter = DA_HEAD_DIM // 4
    inv = ROPE_BASE ** (-jnp.arange(quarter, dtype=F32) / quarter)
    ang_r, ang_c = row[:, None] * inv, col[:, None] * inv

    def half(ang):
        return (jnp.concatenate([jnp.cos(ang), jnp.cos(ang)], -1),
                jnp.concatenate([-jnp.sin(ang), jnp.sin(ang)], -1))

    cr, sr = half(ang_r)
    cc, sc = half(ang_c)
    cos64 = jnp.concatenate([cr, cc], -1)
    sin64 = jnp.concatenate([sr, sc], -1)
    return jnp.tile(cos64, (1, 2)), jnp.tile(sin64, (1, 2))


def _ret_tables(n):
    inv = 1.0 / (ROPE_BASE ** jnp.linspace(0.0, 1.0, RET_QK_DIM // 2, dtype=F32))
    ang = jnp.arange(n, dtype=F32)[:, None] * inv
    cos64 = jnp.concatenate([jnp.cos(ang), jnp.cos(ang)], -1)
    sin64 = jnp.concatenate([-jnp.sin(ang), jnp.sin(ang)], -1)
    return jnp.tile(cos64, (1, 2)), jnp.tile(sin64, (1, 2))


def _ret_decay_tables(log_gamma):
    lc = RET_CHUNK
    pos = jnp.arange(lc, dtype=F32)
    lg = jnp.repeat(log_gamma, RET_QK_DIM, axis=1).reshape(2, 2, 1, V7X_LANES)
    qd = jnp.stack([jnp.exp(lg[0] * (pos + 1.0)[None, :, None]), jnp.exp(lg[1] * (lc - pos)[None, :, None])])
    kd = jnp.stack([jnp.exp(lg[0] * (lc - 1.0 - pos)[None, :, None]), jnp.exp(lg[1] * pos[None, :, None])])
    cd = jnp.broadcast_to(jnp.exp(lg * lc), (2, 2, lc, V7X_LANES))
    dec = jnp.stack([qd, kd, cd], axis=2)
    dist = pos[:, None] - pos[None, :]
    lgh = log_gamma[:, :, None, None]
    fwd = jnp.where(dist >= 0, jnp.exp(lgh[0] * jnp.maximum(dist, 0.0)), 0.0)
    bwd = jnp.where(dist <= 0, jnp.exp(lgh[1] * jnp.maximum(-dist, 0.0)), 0.0)
    intra = jnp.stack([fwd, bwd]).reshape(2, 2, 2, lc, lc)
    return jnp.swapaxes(dec, 0, 1), jnp.swapaxes(intra, 0, 1)


def kernel(x, c, ctx, c_ctx, w_mod, b_mod, norm1_g, norm2_g, w_in, w_out, da_q_norm_g, da_k_norm_g,
           da_lambda_q1, da_lambda_k1, da_lambda_q2, da_lambda_k2, da_subln_g, ret_decay,
           sg_norm_g, sg_w, sg_b, router_w, ex_w_gate, ex_w_up, ex_w_down):
    b, n, d = x.shape
    n_ctx = ctx.shape[1]
    cos_a, sin_a = _attn_tables(n)
    cos_r, sin_r = _ret_tables(n)
    one_c = jnp.ones((n_ctx, V7X_LANES), F32)
    zero_c = jnp.zeros((n_ctx, V7X_LANES), F32)
    lane = jnp.arange(V7X_LANES)
    seg = (lane[:, None] // 64 == lane[None, :] // 64).astype(BF16)
    state0 = jnp.zeros((b, 2, 2, V7X_LANES, V7X_LANES), F32)
    tk = 512

    for layer in range(DEPTH):
        need_ctx = layer < DEPTH - 1
        lambda_init = 0.8 - 0.6 * math.exp(-0.3 * layer)
        mod = jax.nn.silu(c) @ w_mod[layer] + b_mod[layer]
        mod_c = jnp.broadcast_to(jax.nn.silu(c_ctx) @ w_mod[layer] + b_mod[layer], mod.shape)
        sh1, sc1, g1, sh2, sc2, g2 = jnp.split(mod[:, None, :], 6, axis=-1)
        csh1, csc1, cg1, csh2, csc2, cg2 = jnp.split(mod_c[:, None, :], 6, axis=-1)
        lam = (jnp.exp(jnp.sum(da_lambda_q1[layer] * da_lambda_k1[layer]))
               - jnp.exp(jnp.sum(da_lambda_q2[layer] * da_lambda_k2[layer])) + lambda_init).reshape(1)
        log_gamma = jax.nn.log_sigmoid(ret_decay[layer].astype(F32))
        dec, intra = _ret_decay_tables(log_gamma)
        w_in_b = w_in[layer].astype(BF16)
        w_out_b = w_out[layer].astype(BF16)
        rwt = router_w[layer].T.astype(BF16)
        qg = jnp.tile(da_q_norm_g[layer], 2)[None, :]
        kg = jnp.tile(da_k_norm_g[layer], 2)[None, :]
        gcol = (da_subln_g[layer] * (1.0 - lambda_init))[:, None]
        sg_ng = sg_norm_g[layer][None, :]
        sg_wb = sg_w[layer].astype(BF16)
        sg_bm = jnp.repeat(sg_b[layer].T, SG_GROUP_DIM, axis=1)
        n1, n2 = norm1_g[layer][None, None, :], norm2_g[layer][None, None, :]

        z_lat = in_proj(x, n1 * (1.0 + sc1), sh1, w_in_b, 512)
        z_ctx = in_proj(ctx, n1 * (1.0 + csc1), csh1, w_in_b, n_ctx)

        cq1, cq2, ck, cvt = qkv_prep(z_ctx, one_c, zero_c, qg, kg, seg, n_ctx)
        q1, q2, kl, vlt = qkv_prep(z_lat, cos_a, sin_a, qg, kg, seg, tk)
        a_lat = diff_attention(lam, q1, q2, ck, cvt, kl, vlt, gcol, 256)

        cyf, cyb, cstate = retention(z_ctx, one_c, zero_c, dec, intra, seg, state0, n_ctx)
        yf, yb, _ = retention(z_lat, cos_r, sin_r, dec, intra, seg, cstate, 512)

        s_lat = spatial_gating(z_lat, sg_ng, sg_wb, sg_bm, 512)

        x, f_lat, aff_lat = out_proj(a_lat, yf, yb, s_lat, x, g1, n2 * (1.0 + sc2), sh2, w_out_b, rwt, 512)
        x = x + g2 * expert_choice(f_lat, aff_lat, ex_w_gate[layer], ex_w_up[layer], ex_w_down[layer])
        if need_ctx:
            a_ctx = diff_attention(lam, cq1, cq2, ck, cvt, None, None, gcol, n_ctx)
            s_ctx = spatial_gating(z_ctx, sg_ng, sg_wb, sg_bm, n_ctx)
            ctx, f_ctx, aff_ctx = out_proj(a_ctx, cyf, cyb, s_ctx, ctx, cg1, n2 * (1.0 + csc2), csh2,
                                           w_out_b, rwt, n_ctx)
            ctx = ctx + cg2 * expert_choice(f_ctx, aff_ctx, ex_w_gate[layer], ex_w_up[layer], ex_w_down[layer])
    return x
```

```python
import functools
import math

import jax
import jax.numpy as jnp
from jax import lax
from jax.experimental import pallas as pl
from jax.experimental.pallas import tpu as pltpu

F32 = jnp.float32
BF16 = jnp.bfloat16

D_MODEL = 1024
DEPTH = 4
GRID_W = 64
NORM_EPS = 1e-6
ROPE_BASE = 10000.0
DA_HEADS = 4
DA_HEAD_DIM = 64
RET_HEADS = 4
RET_QK_DIM = 64
RET_CHUNK = 128
SG_GROUPS = 4
SG_GROUP_DIM = 64
SG_CHUNK = 128
N_EXPERTS = 16
EXPERT_FF = 2 * D_MODEL
EC_CAPACITY = 2
IN_COLS = 3328
COL_AQ, COL_AK, COL_AV = 0, 4, 8
COL_RQ, COL_RK, COL_RV, COL_RGF, COL_RGB = 12, 14, 16, 18, 20
COL_SU256, COL_SV256 = 11, 12

V7X_LANES = 128
V7X_BF16_SUBLANES = 16
V7X_VMEM_BYTES = 64 * 1024 * 1024
MOE_TOKEN_BLOCK = 256
ATTN_BLOCKS_PER_STEP = 3
NEG_BIG = -1e30
LOG2_E = 1.4426950408889634
F32_EXP2_RANGE = 120.0


def _cparams(semantics, vmem_mib):
    assert vmem_mib * 1024 * 1024 < V7X_VMEM_BYTES
    return pltpu.CompilerParams(dimension_semantics=semantics, vmem_limit_bytes=vmem_mib * 1024 * 1024)


def _dot(a, b):
    return jnp.dot(a, b, preferred_element_type=F32)


def _dot_nt(a, b):
    return lax.dot_general(a, b, (((1,), (1,)), ((), ())), preferred_element_type=F32)


def _dot_tn(a, b):
    return lax.dot_general(a, b, (((0,), (0,)), ((), ())), preferred_element_type=F32)


def _seg_dot(x, seg):
    hi = x.astype(BF16)
    lo = (x - hi.astype(F32)).astype(BF16)
    return _dot(hi, seg) + _dot(lo, seg)


def _sigmoid(x):
    return 1.0 / (1.0 + jnp.exp(-x))


def _in_proj_body(x_ref, a_ref, s_ref, w_ref, z_ref):
    x = x_ref[...]
    ms = jnp.mean(x * x, axis=-1, keepdims=True)
    h = x * lax.rsqrt(ms + NORM_EPS) * a_ref[...] + s_ref[...]
    z_ref[...] = _dot(h.astype(BF16), w_ref[...])


def in_proj(x, a, s, w, tm):
    b, r, d = x.shape
    c = w.shape[1]
    return pl.pallas_call(
        _in_proj_body,
        grid=(b, r // tm),
        in_specs=[
            pl.BlockSpec((None, tm, d), lambda i, j: (i, j, 0)),
            pl.BlockSpec((None, 1, d), lambda i, j: (i, 0, 0)),
            pl.BlockSpec((None, 1, d), lambda i, j: (i, 0, 0)),
            pl.BlockSpec((d, c), lambda i, j: (0, 0)),
        ],
        out_specs=pl.BlockSpec((None, tm, c), lambda i, j: (i, j, 0)),
        out_shape=jax.ShapeDtypeStruct((b, r, c), F32),
        compiler_params=_cparams(("parallel", "parallel"), 48),
        name="in_proj",
    )(x, a, s, w)


def _qkv_prep_body(q_ref, k_ref, v_ref, cos_ref, sin_ref, qg_ref, kg_ref, seg_ref,
                   qp1_ref, qp2_ref, kk_ref, vt_ref):
    lane = lax.broadcasted_iota(jnp.int32, (1, V7X_LANES), 1)
    lower16 = (lane % 32) < 16
    seg = seg_ref[...]
    cos = cos_ref[...]
    sin = sin_ref[...]

    def norm_rope(x, g):
        ms = _seg_dot(x * x, seg) * (1.0 / DA_HEAD_DIM)
        y = x * lax.rsqrt(ms + NORM_EPS) * g
        partner = jnp.where(lower16, pltpu.roll(y, 112, 1), pltpu.roll(y, 16, 1))
        return y * cos + partner * sin

    q = norm_rope(q_ref[...], qg_ref[...]) * (DA_HEAD_DIM ** -0.5 * LOG2_E)
    first = lane < DA_HEAD_DIM
    qp1_ref[...] = jnp.where(first, q, 0.0).astype(BF16)
    qp2_ref[...] = jnp.where(first, 0.0, q).astype(BF16)
    kk_ref[...] = norm_rope(k_ref[...], kg_ref[...]).astype(BF16)
    vt_ref[...] = v_ref[...].T.astype(BF16)


def qkv_prep(z, cos, sin, qg, kg, seg, tm):
    b, r, _ = z.shape
    h = DA_HEADS
    row = lambda off: pl.BlockSpec((None, tm, V7X_LANES), lambda i, j, k: (i, k, off + j))
    tab = pl.BlockSpec((tm, V7X_LANES), lambda i, j, k: (k, 0))
    vec = pl.BlockSpec((1, V7X_LANES), lambda i, j, k: (0, 0))
    o4 = pl.BlockSpec((None, None, tm, V7X_LANES), lambda i, j, k: (i, j, k, 0))
    sd = jax.ShapeDtypeStruct((b, h, r, V7X_LANES), BF16)
    return pl.pallas_call(
        _qkv_prep_body,
        grid=(b, h, r // tm),
        in_specs=[row(COL_AQ), row(COL_AK), row(COL_AV), tab, tab, vec, vec,
                  pl.BlockSpec((V7X_LANES, V7X_LANES), lambda i, j, k: (0, 0))],
        out_specs=[o4, o4, o4,
                   pl.BlockSpec((None, None, None, V7X_LANES, tm), lambda i, j, k: (i, j, k, 0, 0))],
        out_shape=[sd, sd, sd, jax.ShapeDtypeStruct((b, h, r // tm, V7X_LANES, tm), BF16)],
        compiler_params=_cparams(("parallel", "parallel", "parallel"), 32),
        name="qkv_prep",
    )(z, z, z, cos, sin, qg, kg, seg)


def _attn_body(lam_ref, q1_ref, q2_ref, c1_ref, c2_ref, k_ref, v_ref, g_ref, o_ref, acc_ref, *, n_steps, bps):
    q1 = q1_ref[...]
    q2 = q2_ref[...]
    c1 = c1_ref[...]
    c2 = c2_ref[...]
    tq = q1.shape[0]
    tk = v_ref.shape[2]
    keys = bps * tk

    def probs(i):
        k = k_ref[pl.ds(pl.multiple_of(i * keys, keys), keys), :]
        p1 = jnp.exp2(_dot_nt(k, q1) - c1)
        p2 = jnp.exp2(_dot_nt(k, q2) - c2)
        return (p1.astype(BF16), p2.astype(BF16),
                jnp.sum(p1.reshape(-1, 8, tq), axis=0), jnp.sum(p2.reshape(-1, 8, tq), axis=0))

    def accumulate(i, p1, p2):
        for idx, p in ((0, p1), (1, p2)):
            part = _dot(v_ref[i * bps], p[0:tk])
            for s in range(1, bps):
                part = part + _dot(v_ref[i * bps + s], p[s * tk:(s + 1) * tk])
            acc_ref[idx] += part

    acc_ref[...] = jnp.zeros(acc_ref.shape, F32)
    carry = probs(0)

    def body(i, c):
        accumulate(i, c[0], c[1])
        n1, n2, d1, d2 = probs(i + 1)
        return n1, n2, c[2] + d1, c[3] + d2

    carry = lax.fori_loop(0, n_steps - 1, body, carry)
    accumulate(n_steps - 1, carry[0], carry[1])
    l1 = jnp.sum(carry[2], axis=0, keepdims=True)
    l2 = jnp.sum(carry[3], axis=0, keepdims=True)
    o = acc_ref[0] / l1 - lam_ref[0] * (acc_ref[1] / l2)
    ms = jnp.mean(o * o, axis=0, keepdims=True)
    y = o * lax.rsqrt(ms + NORM_EPS) * g_ref[...]
    o_ref[...] = y.T.astype(BF16)


def diff_attention(lam, q1, q2, c1, c2, k, vt, gcol, tq, bps):
    b, h, nq, _ = q1.shape
    nk = k.shape[2]
    nblk, _, tk = vt.shape[2:]
    qs = pl.BlockSpec((None, None, tq, V7X_LANES), lambda i, j, s: (i, j, s, 0))
    cs = pl.BlockSpec((None, None, 1, tq), lambda i, j, s: (i, j, 0, s))
    return pl.pallas_call(
        functools.partial(_attn_body, n_steps=nblk // bps, bps=bps),
        grid=(b, h, nq // tq),
        in_specs=[pl.BlockSpec(memory_space=pltpu.SMEM), qs, qs, cs, cs,
                  pl.BlockSpec((None, None, nk, V7X_LANES), lambda i, j, s: (i, j, 0, 0)),
                  pl.BlockSpec((None, None, nblk, V7X_LANES, tk), lambda i, j, s: (i, j, 0, 0, 0)),
                  pl.BlockSpec((V7X_LANES, 1), lambda i, j, s: (0, 0))],
        out_specs=pl.BlockSpec((None, tq, V7X_LANES), lambda i, j, s: (i, s, j)),
        out_shape=jax.ShapeDtypeStruct((b, nq, h * V7X_LANES), BF16),
        scratch_shapes=[pltpu.VMEM((2, V7X_LANES, tq), F32)],
        compiler_params=_cparams(("parallel", "parallel", "parallel"), 48),
        name="diff_attention",
    )(lam, q1, q2, c1, c2, k, vt, gcol)


def _rowmax_body(q1_ref, q2_ref, k_ref, m1_ref, m2_ref, *, n_blocks, tk):
    q1 = q1_ref[...]
    q2 = q2_ref[...]

    def body(i, carry):
        k = k_ref[pl.ds(pl.multiple_of(i * tk, tk), tk), :]
        return (jnp.maximum(carry[0], jnp.max(_dot_nt(k, q1), axis=0, keepdims=True)),
                jnp.maximum(carry[1], jnp.max(_dot_nt(k, q2), axis=0, keepdims=True)))

    start = jnp.full((1, q1.shape[0]), NEG_BIG, F32)
    m1, m2 = lax.fori_loop(0, n_blocks, body, (start, start))
    m1_ref[...] = m1
    m2_ref[...] = m2


def score_rowmax(q1, q2, k, tq, tk):
    b, h, nq, _ = q1.shape
    nk = k.shape[2]
    qs = pl.BlockSpec((None, None, tq, V7X_LANES), lambda i, j, s: (i, j, s, 0))
    ms = pl.BlockSpec((None, None, 1, tq), lambda i, j, s: (i, j, 0, s))
    sd = jax.ShapeDtypeStruct((b, h, 1, nq), F32)
    return pl.pallas_call(
        functools.partial(_rowmax_body, n_blocks=nk // tk, tk=tk),
        grid=(b, h, nq // tq),
        in_specs=[qs, qs, pl.BlockSpec((None, None, nk, V7X_LANES), lambda i, j, s: (i, j, 0, 0))],
        out_specs=[ms, ms],
        out_shape=[sd, sd],
        compiler_params=_cparams(("parallel", "parallel", "parallel"), 32),
        name="score_rowmax",
    )(q1, q2, k)


def softmax_shifts(q1, q2, k, tq, tk):
    def sq(t, lo, hi):
        return jnp.sum(jnp.square(t[..., lo:hi].astype(F32)), axis=-1)

    half = DA_HEAD_DIM
    k1 = jnp.max(sq(k, 0, half), axis=-1)
    k2 = jnp.max(sq(k, half, 2 * half), axis=-1)
    b1 = jnp.sqrt(sq(q1, 0, 2 * half) * k1[..., None])[:, :, None, :]
    b2 = jnp.sqrt(sq(q2, 0, 2 * half) * k2[..., None])[:, :, None, :]
    safe = 2.0 * jnp.maximum(jnp.max(b1), jnp.max(b2)) <= F32_EXP2_RANGE
    return lax.cond(safe, lambda: (b1, b2), lambda: tuple(score_rowmax(q1, q2, k, tq, tk)))


def _ret_body(qf_ref, kf_ref, vf_ref, gf_ref, cosf_ref, sinf_ref,
              qb_ref, kb_ref, vb_ref, gb_ref, cosb_ref, sinb_ref,
              dec_ref, intra_ref, seg_ref, s0_ref,
              yf_ref, yb_ref, sout_ref, state_ref, *, n_chunks):
    step = pl.program_id(2)
    lc = RET_CHUNK

    @pl.when(step == 0)
    def _():
        state_ref[...] = s0_ref[...]

    lane = lax.broadcasted_iota(jnp.int32, (1, V7X_LANES), 1)
    lower32 = (lane % RET_QK_DIM) < (RET_QK_DIM // 2)
    head0 = lane < RET_QK_DIM
    blk = (lax.broadcasted_iota(jnp.int32, (V7X_LANES, V7X_LANES), 0) // RET_QK_DIM ==
           lax.broadcasted_iota(jnp.int32, (V7X_LANES, V7X_LANES), 1) // RET_QK_DIM)
    seg = seg_ref[...]

    def rot(x, cos, sin):
        partner = jnp.where(lower32, pltpu.roll(x, 96, 1), pltpu.roll(x, 32, 1))
        return x * cos + partner * sin

    def chunk(d, q_ref, k_ref, v_ref, g_ref, cos_ref, sin_ref, y_ref, c):
        rows = pl.ds(c * lc, lc)
        cos = cos_ref[rows, :]
        sin = sin_ref[rows, :]
        q = rot(q_ref[rows, :], cos, sin)
        k = rot(k_ref[rows, :], cos, sin) * (RET_QK_DIM ** -0.5)
        v = v_ref[rows, :]
        qdec = dec_ref[d, 0]
        kdec = dec_ref[d, 1]
        cdec = dec_ref[d, 2][0:1, :]
        kb = k.astype(BF16)
        vb = v.astype(BF16)
        s = state_ref[d]
        o = _dot(q.astype(BF16), s.astype(BF16)) * qdec
        for hh in range(2):
            sel = head0 if hh == 0 else jnp.logical_not(head0)
            qm = jnp.where(sel, q, 0.0).astype(BF16)
            sc = _dot_nt(qm, kb) * intra_ref[d, hh]
            o = o + _dot(sc.astype(BF16), jnp.where(sel, v, 0.0).astype(BF16))
        kv = _dot_tn((k * kdec).astype(BF16), vb)
        state_ref[d] = s * cdec + jnp.where(blk, kv, 0.0)
        mu = _seg_dot(o, seg) * (1.0 / RET_QK_DIM)
        cen = o - mu
        var = _seg_dot(cen * cen, seg) * (1.0 / RET_QK_DIM)
        g = g_ref[rows, :]
        y_ref[rows, :] = cen * lax.rsqrt(var + NORM_EPS) * (g * _sigmoid(g))

    for c in range(n_chunks):
        chunk(0, qf_ref, kf_ref, vf_ref, gf_ref, cosf_ref, sinf_ref, yf_ref, c)
    for c in reversed(range(n_chunks)):
        chunk(1, qb_ref, kb_ref, vb_ref, gb_ref, cosb_ref, sinb_ref, yb_ref, c)

    @pl.when(step == pl.num_programs(2) - 1)
    def _():
        sout_ref[...] = state_ref[...]


def retention(z, cos, sin, dec, intra, seg, s0, tm):
    b, r, _ = z.shape
    nb = r // tm
    fwd = lambda off: pl.BlockSpec((None, tm, V7X_LANES), lambda i, p, k: (i, k, off + p))
    bwd = lambda off: pl.BlockSpec((None, tm, V7X_LANES), lambda i, p, k: (i, nb - 1 - k, off + p))
    tabf = pl.BlockSpec((tm, V7X_LANES), lambda i, p, k: (k, 0))
    tabb = pl.BlockSpec((tm, V7X_LANES), lambda i, p, k: (nb - 1 - k, 0))
    st = pl.BlockSpec((None, None, 2, V7X_LANES, V7X_LANES), lambda i, p, k: (i, p, 0, 0, 0))
    ysd = jax.ShapeDtypeStruct((b, r, 2 * V7X_LANES), F32)
    return pl.pallas_call(
        functools.partial(_ret_body, n_chunks=tm // RET_CHUNK),
        grid=(b, 2, nb),
        in_specs=[fwd(COL_RQ), fwd(COL_RK), fwd(COL_RV), fwd(COL_RGF), tabf, tabf,
                  bwd(COL_RQ), bwd(COL_RK), bwd(COL_RV), bwd(COL_RGB), tabb, tabb,
                  pl.BlockSpec((None, 2, 3, RET_CHUNK, V7X_LANES), lambda i, p, k: (p, 0, 0, 0, 0)),
                  pl.BlockSpec((None, 2, 2, RET_CHUNK, RET_CHUNK), lambda i, p, k: (p, 0, 0, 0, 0)),
                  pl.BlockSpec((V7X_LANES, V7X_LANES), lambda i, p, k: (0, 0)),
                  st],
        out_specs=[pl.BlockSpec((None, tm, V7X_LANES), lambda i, p, k: (i, k, p)),
                   pl.BlockSpec((None, tm, V7X_LANES), lambda i, p, k: (i, nb - 1 - k, p)),
                   st],
        out_shape=[ysd, ysd, jax.ShapeDtypeStruct(s0.shape, F32)],
        scratch_shapes=[pltpu.VMEM((2, V7X_LANES, V7X_LANES), F32)],
        compiler_params=_cparams(("parallel", "parallel", "arbitrary"), 32),
        name="retention",
    )(z, z, z, z, cos, sin, z, z, z, z, cos, sin, dec, intra, seg, s0)


def _sg_body(su_ref, sv_ref, ng_ref, w_ref, bm_ref, o_ref):
    tm = su_ref.shape[0]
    u = jax.nn.gelu(su_ref[...])
    t = jax.nn.gelu(sv_ref[...])
    mu = jnp.mean(t, axis=-1, keepdims=True)
    cen = t - mu
    var = jnp.mean(cen * cen, axis=-1, keepdims=True)
    v = (cen * lax.rsqrt(var + NORM_EPS) * ng_ref[...]).astype(BF16)
    group = lax.broadcasted_iota(jnp.int32, (1, 2 * V7X_LANES), 1) // SG_GROUP_DIM
    bias = bm_ref[...]
    for c in range(tm // SG_CHUNK):
        vc = v[c * SG_CHUNK:(c + 1) * SG_CHUNK, :]
        s = bias
        for g in range(SG_GROUPS):
            s = s + jnp.where(group == g, _dot(w_ref[g], vc), 0.0)
        o_ref[c * SG_CHUNK:(c + 1) * SG_CHUNK, :] = (u[c * SG_CHUNK:(c + 1) * SG_CHUNK, :] * s).astype(BF16)


def spatial_gating(z, ng, w, bm, tm):
    b, r, _ = z.shape
    wide = 2 * V7X_LANES
    return pl.pallas_call(
        _sg_body,
        grid=(b, r // tm),
        in_specs=[pl.BlockSpec((None, tm, wide), lambda i, j: (i, j, COL_SU256)),
                  pl.BlockSpec((None, tm, wide), lambda i, j: (i, j, COL_SV256)),
                  pl.BlockSpec((1, wide), lambda i, j: (0, 0)),
                  pl.BlockSpec((SG_GROUPS, SG_CHUNK, SG_CHUNK), lambda i, j: (0, 0, 0)),
                  pl.BlockSpec((SG_CHUNK, wide), lambda i, j: (0, 0))],
        out_specs=pl.BlockSpec((None, tm, wide), lambda i, j: (i, j, 0)),
        out_shape=jax.ShapeDtypeStruct((b, r, wide), BF16),
        compiler_params=_cparams(("parallel", "parallel"), 32),
        name="spatial_gating",
    )(z, z, ng, w, bm)


def _out_proj_body(a_ref, yf_ref, yb_ref, sg_ref, x_ref, g1_ref, a2_ref, s2_ref, w_ref, rwt_ref,
                   xo_ref, f_ref, aff_ref):
    na = a_ref.shape[1]
    nr = yf_ref.shape[1]
    r = (yf_ref[...] + yb_ref[...]).astype(BF16)
    y = (_dot(a_ref[...], w_ref[0:na, :]) + _dot(r, w_ref[na:na + nr, :])
         + _dot(sg_ref[...], w_ref[na + nr:, :]))
    xn = x_ref[...] + g1_ref[...] * y
    xo_ref[...] = xn
    ms = jnp.mean(xn * xn, axis=-1, keepdims=True)
    f = (xn * lax.rsqrt(ms + NORM_EPS) * a2_ref[...] + s2_ref[...]).astype(BF16)
    f_ref[...] = f
    logits = _dot_nt(rwt_ref[...], f)
    e = jnp.exp(logits - jnp.max(logits, axis=0, keepdims=True))
    aff_ref[...] = e / jnp.sum(e, axis=0, keepdims=True)


def out_proj(a, yf, yb, sg, x, g1, a2, s2, w, rwt, tm):
    b, r, d = x.shape
    rows = lambda width: pl.BlockSpec((None, tm, width), lambda i, j: (i, j, 0))
    vec = pl.BlockSpec((None, 1, d), lambda i, j: (i, 0, 0))
    return pl.pallas_call(
        _out_proj_body,
        grid=(b, r // tm),
        in_specs=[rows(a.shape[2]), rows(yf.shape[2]), rows(yb.shape[2]), rows(sg.shape[2]), rows(d),
                  vec, vec, vec,
                  pl.BlockSpec(w.shape, lambda i, j: (0, 0)),
                  pl.BlockSpec(rwt.shape, lambda i, j: (0, 0))],
        out_specs=[rows(d), rows(d), pl.BlockSpec((None, N_EXPERTS, tm), lambda i, j: (i, 0, j))],
        out_shape=[jax.ShapeDtypeStruct((b, r, d), F32), jax.ShapeDtypeStruct((b, r, d), BF16),
                   jax.ShapeDtypeStruct((b, N_EXPERTS, r), F32)],
        compiler_params=_cparams(("parallel", "parallel"), 32),
        name="out_proj",
    )(a, yf, yb, sg, x, g1, a2, s2, w, rwt)


def _router_body(aff_ref, tri_ref, posm_ref, off_ref, *, cap, tb):
    aff = aff_ref[...]
    e, r = aff.shape
    nb = r // tb
    bits = pltpu.bitcast(aff, jnp.int32)

    def search(i, thr):
        cand = thr | lax.shift_left(jnp.int32(1), 30 - i)
        cnt = jnp.sum(jnp.where(bits >= cand, 1.0, 0.0), axis=1, keepdims=True)
        return jnp.where(cnt >= cap, cand, thr)

    thr = lax.fori_loop(0, 31, search, jnp.zeros((e, 1), jnp.int32))
    gt = jnp.where(bits > thr, 1.0, 0.0)
    eq = jnp.where(bits == thr, 1.0, 0.0)
    need = cap - jnp.sum(gt, axis=1, keepdims=True)
    tri = tri_ref[...]
    lane = lax.broadcasted_iota(jnp.int32, (e, V7X_LANES), 1)
    offs = jnp.zeros((e, V7X_LANES), jnp.int32)
    run_eq = jnp.zeros((e, 1), F32)
    run_sel = jnp.zeros((e, 1), F32)
    for j in range(nb):
        cols = slice(j * tb, (j + 1) * tb)
        eq_j = eq[:, cols]
        before_eq = _dot(eq_j.astype(BF16), tri) + run_eq
        sel_j = gt[:, cols] + eq_j * jnp.where(before_eq < need, 1.0, 0.0)
        pos_j = _dot(sel_j.astype(BF16), tri) + run_sel
        posm_ref[:, cols] = jnp.where(sel_j > 0.0, pos_j, -1.0)
        offs = jnp.where(lane == j, run_sel.astype(jnp.int32), offs)
        run_eq = run_eq + jnp.sum(eq_j, axis=1, keepdims=True)
        run_sel = run_sel + jnp.sum(sel_j, axis=1, keepdims=True)
    off_ref[...] = jnp.where(lane == nb, run_sel.astype(jnp.int32), offs)


def route(aff_t, tri, cap):
    b, e, r = aff_t.shape
    tb = tri.shape[0]
    posm, off = pl.pallas_call(
        functools.partial(_router_body, cap=cap, tb=tb),
        grid=(b,),
        in_specs=[pl.BlockSpec((None, e, r), lambda i: (i, 0, 0)),
                  pl.BlockSpec((tb, tb), lambda i: (0, 0))],
        out_specs=[pl.BlockSpec((None, e, r), lambda i: (i, 0, 0)),
                   pl.BlockSpec((None, e, V7X_LANES), lambda i: (i, 0, 0))],
        out_shape=[jax.ShapeDtypeStruct((b, e, r), F32), jax.ShapeDtypeStruct((b, e, V7X_LANES), jnp.int32)],
        compiler_params=_cparams(("parallel",), 32),
        name="router",
    )(aff_t, tri)
    shape5 = (b, e, r // tb, 1, tb)
    return posm.reshape(shape5), aff_t.reshape(shape5), off.reshape(-1)


def _gather_body(off_ref, f_hbm, posm_ref, aff_ref, x_ref, g_ref, fbuf, xacc, gacc, sem, *, cap, qc, tb, nb):
    b = pl.program_id(0)
    e = pl.program_id(1)

    @pl.when(e == 0)
    def _():
        cp = pltpu.make_async_copy(f_hbm.at[b], fbuf, sem)
        cp.start()
        cp.wait()

    base = (b * pl.num_programs(1) + e) * V7X_LANES
    slot = lax.broadcasted_iota(jnp.int32, (qc, 1), 0).astype(F32)

    def chunk(q, j0):
        lo = jnp.asarray(q * qc, jnp.int32)
        xacc[...] = jnp.zeros(xacc.shape, F32)
        gacc[...] = jnp.zeros(gacc.shape, F32)

        def body(carry):
            j = carry[0]

            @pl.when(off_ref[base + j + 1] > lo)
            def _():
                onehot = posm_ref[j] == slot + lo.astype(F32)
                rows = pl.ds(pl.multiple_of(j * tb, tb), tb)
                xacc[...] += _dot(jnp.where(onehot, 1.0, 0.0).astype(BF16), fbuf[rows, :])
                gacc[...] += jnp.sum(jnp.where(onehot, aff_ref[j], 0.0), axis=1, keepdims=True)

            return j + 1, off_ref[base + j + 1]

        jend, _ = lax.while_loop(lambda c: jnp.logical_and(c[0] < nb, c[1] < lo + qc), body,
                                 (j0, off_ref[base + j0]))
        rows = pl.ds(pl.multiple_of(lo, qc), qc)
        x_ref[rows, :] = xacc[...].astype(BF16)
        g_ref[rows, :] = gacc[...]
        return jnp.maximum(jend - 1, 0)

    lax.fori_loop(0, cap // qc, chunk, jnp.int32(0))


def moe_gather(off, f, posm, aff, cap, qc):
    b, r, d = f.shape
    _, e, nb, _, tb = posm.shape
    blk = pl.BlockSpec((None, None, nb, 1, tb), lambda i, j, o: (i, j, 0, 0, 0))
    return pl.pallas_call(
        functools.partial(_gather_body, cap=cap, qc=qc, tb=tb, nb=nb),
        grid_spec=pltpu.PrefetchScalarGridSpec(
            num_scalar_prefetch=1,
            grid=(b, e),
            in_specs=[pl.BlockSpec(memory_space=pl.ANY), blk, blk],
            out_specs=[pl.BlockSpec((None, None, cap, d), lambda i, j, o: (i, j, 0, 0)),
                       pl.BlockSpec((None, None, cap, 1), lambda i, j, o: (i, j, 0, 0))],
            scratch_shapes=[pltpu.VMEM((r, d), BF16), pltpu.VMEM((qc, d), F32), pltpu.VMEM((qc, 1), F32),
                            pltpu.SemaphoreType.DMA(())]),
        out_shape=[jax.ShapeDtypeStruct((b, e, cap, d), BF16), jax.ShapeDtypeStruct((b, e, cap, 1), F32)],
        compiler_params=_cparams(("arbitrary", "arbitrary"), 40),
        name="moe_gather",
    )(off, f, posm, aff)


def _moe_ffn_body(*refs, has_ctx, cap, capc):
    if has_ctx:
        _, x_ref, g_ref, xc_ref, gc_ref, wg_ref, wu_ref, wd_ref, y_ref, yc_ref, yacc, ycacc = refs
    else:
        _, x_ref, g_ref, wg_ref, wu_ref, wd_ref, y_ref, yacc = refs
    c = pl.program_id(1)
    last = pl.num_programs(1) - 1
    nbatch, _, d = x_ref.shape
    wg = wg_ref[...].astype(BF16)
    wu = wu_ref[...].astype(BF16)
    wd = wd_ref[...].astype(BF16)

    def swiglu(x):
        a = _dot(x, wg)
        u = _dot(x, wu)
        return _dot((a * _sigmoid(a) * u).astype(BF16), wd)

    def accumulate(acc, idx, part):
        @pl.when(c == 0)
        def _():
            acc[idx] = part

        @pl.when(c > 0)
        def _():
            acc[idx] += part

    for i in range(nbatch):
        accumulate(yacc, i, swiglu(x_ref[i]))
    if has_ctx:
        accumulate(ycacc, slice(None), swiglu(xc_ref[...].reshape(nbatch * capc, d)))

    @pl.when(c == last)
    def _():
        for i in range(nbatch):
            y_ref[i, 0:cap, :] = (yacc[i] * g_ref[i]).astype(BF16)
            y_ref[i, cap:, :] = jnp.zeros((y_ref.shape[1] - cap, d), BF16)
        if has_ctx:
            yc = ycacc[...] * gc_ref[...].reshape(nbatch * capc, 1)
            yc_ref[:, 0:capc, :] = yc.reshape(nbatch, capc, d).astype(BF16)
            yc_ref[:, capc:, :] = jnp.zeros((nbatch, yc_ref.shape[1] - capc, d), BF16)


def moe_ffn(layer, x, g, xc, gc, wg, wu, wd, tf, pad, padc):
    b, e, cap, d = x.shape
    ff = wg.shape[3]
    has_ctx = xc is not None
    capc = xc.shape[2] if has_ctx else 0
    tok = lambda rows, width: pl.BlockSpec((b, None, rows, width), lambda j, k, l: (0, j, 0, 0))
    in_specs = [tok(cap, d), tok(cap, 1)]
    out_specs = [tok(cap + pad, d)]
    out_shape = [jax.ShapeDtypeStruct((b, e, cap + pad, d), BF16)]
    scratch = [pltpu.VMEM((b, cap, d), F32)]
    args = [x, g]
    if has_ctx:
        in_specs += [tok(capc, d), tok(capc, 1)]
        out_specs.append(tok(capc + padc, d))
        out_shape.append(jax.ShapeDtypeStruct((b, e, capc + padc, d), BF16))
        scratch.append(pltpu.VMEM((b * capc, d), F32))
        args += [xc, gc]
    in_specs += [pl.BlockSpec((None, None, d, tf), lambda j, k, l: (l[0], j, 0, k)),
                 pl.BlockSpec((None, None, d, tf), lambda j, k, l: (l[0], j, 0, k)),
                 pl.BlockSpec((None, None, tf, d), lambda j, k, l: (l[0], j, k, 0))]
    out = pl.pallas_call(
        functools.partial(_moe_ffn_body, has_ctx=has_ctx, cap=cap, capc=capc),
        grid_spec=pltpu.PrefetchScalarGridSpec(
            num_scalar_prefetch=1, grid=(e, ff // tf),
            in_specs=in_specs, out_specs=out_specs, scratch_shapes=scratch),
        out_shape=out_shape,
        compiler_params=_cparams(("parallel", "arbitrary"), 56),
        name="moe_ffn",
    )(layer, *args, wg, wu, wd)
    return out if has_ctx else (out[0], None)


def _scatter_body(off_ref, x_ref, g2_ref, posm_ref, y_hbm, xo_ref, ybuf, pcat, sem, *, cap, w, nb, n_exp):
    b = pl.program_id(0)
    j = pl.program_id(1)
    step = b * nb + j
    slot = lax.rem(step, 2)
    spare = 2
    align = V7X_BF16_SUBLANES

    def window(bb, jj, e, rnd):
        off = off_ref[(bb * n_exp + e) * V7X_LANES + jj]
        return jnp.minimum((off // align) * align + rnd * w, cap)

    def copies(bb, jj, buf, rnd):
        return [pltpu.make_async_copy(
            y_hbm.at[bb, e, pl.ds(pl.multiple_of(window(bb, jj, e, rnd), align), w), :],
            ybuf.at[buf, pl.ds(e * w, w), :], sem.at[buf]) for e in range(n_exp)]

    @pl.when(step == 0)
    def _():
        for cp in copies(b, j, 0, 0):
            cp.start()

    @pl.when(step + 1 < pl.num_programs(0) * nb)
    def _():
        nxt = step + 1
        for cp in copies(nxt // nb, lax.rem(nxt, nb), 1 - slot, 0):
            cp.start()

    iota = lax.broadcasted_iota(jnp.int32, (w, 1), 0).astype(F32)

    def onehots(rnd):
        for e in range(n_exp):
            first = window(b, j, e, rnd).astype(F32)
            pcat[e * w:(e + 1) * w, :] = jnp.where(posm_ref[e] == first + iota, 1.0, 0.0).astype(BF16)

    for cp in copies(b, j, slot, 0):
        cp.wait()
    onehots(0)
    g2 = g2_ref[...]
    xo_ref[...] = x_ref[...] + g2 * _dot_tn(pcat[...], ybuf[slot])

    span = jnp.int32(0)
    for e in range(n_exp):
        idx = (b * n_exp + e) * V7X_LANES + j
        span = jnp.maximum(span, off_ref[idx + 1] - (off_ref[idx] // align) * align)

    def extra(rnd, carry):
        for cp in copies(b, j, spare, rnd):
            cp.start()
        for cp in copies(b, j, spare, rnd):
            cp.wait()
        onehots(rnd)
        xo_ref[...] += g2 * _dot_tn(pcat[...], ybuf[spare])
        return carry

    lax.fori_loop(1, (span + w - 1) // w, extra, 0)


def moe_scatter(off, x, g2, posm, y, cap, w):
    b, r, d = x.shape
    _, e, nb, _, tb = posm.shape
    rows = pl.BlockSpec((None, tb, d), lambda i, j, o: (i, j, 0))
    return pl.pallas_call(
        functools.partial(_scatter_body, cap=cap, w=w, nb=nb, n_exp=e),
        grid_spec=pltpu.PrefetchScalarGridSpec(
            num_scalar_prefetch=1,
            grid=(b, nb),
            in_specs=[rows, pl.BlockSpec((None, 1, d), lambda i, j, o: (i, 0, 0)),
                      pl.BlockSpec((None, e, None, 1, tb), lambda i, j, o: (i, 0, j, 0, 0)),
                      pl.BlockSpec(memory_space=pl.ANY)],
            out_specs=rows,
            scratch_shapes=[pltpu.VMEM((3, e * w, d), BF16), pltpu.VMEM((e * w, tb), BF16),
                            pltpu.SemaphoreType.DMA((3,))]),
        out_shape=jax.ShapeDtypeStruct((b, r, d), F32),
        compiler_params=_cparams(("arbitrary", "arbitrary"), 32),
        name="moe_scatter",
    )(off, x, g2, posm, y)


def _attn_tables(n):
    rows = n // GRID_W
    row = jnp.broadcast_to(jnp.arange(rows, dtype=F32)[:, None], (rows, GRID_W)).reshape(-1)
    col = jnp.broadcast_to(jnp.arange(GRID_W, dtype=F32)[None, :], (rows, GRID_W)).reshape(-1)
    quarter = DA_HEAD_DIM // 4
    inv = ROPE_BASE ** (-jnp.arange(quarter, dtype=F32) / quarter)
    ang_r, ang_c = row[:, None] * inv, col[:, None] * inv

    def half(ang):
        return (jnp.concatenate([jnp.cos(ang), jnp.cos(ang)], -1),
                jnp.concatenate([-jnp.sin(ang), jnp.sin(ang)], -1))

    cr, sr = half(ang_r)
    cc, sc = half(ang_c)
    cos64 = jnp.concatenate([cr, cc], -1)
    sin64 = jnp.concatenate([sr, sc], -1)
    return jnp.tile(cos64, (1, 2)), jnp.tile(sin64, (1, 2))


def _ret_tables(n):
    inv = 1.0 / (ROPE_BASE ** jnp.linspace(0.0, 1.0, RET_QK_DIM // 2, dtype=F32))
    ang = jnp.arange(n, dtype=F32)[:, None] * inv
    cos64 = jnp.concatenate([jnp.cos(ang), jnp.cos(ang)], -1)
    sin64 = jnp.concatenate([-jnp.sin(ang), jnp.sin(ang)], -1)
    return jnp.tile(cos64, (1, 2)), jnp.tile(sin64, (1, 2))


def _ret_decay_tables(log_gamma):
    lc = RET_CHUNK
    pos = jnp.arange(lc, dtype=F32)
    lg = jnp.repeat(log_gamma, RET_QK_DIM, axis=1).reshape(2, 2, 1, V7X_LANES)
    qd = jnp.stack([jnp.exp(lg[0] * (pos + 1.0)[None, :, None]), jnp.exp(lg[1] * (lc - pos)[None, :, None])])
    kd = jnp.stack([jnp.exp(lg[0] * (lc - 1.0 - pos)[None, :, None]), jnp.exp(lg[1] * pos[None, :, None])])
    cd = jnp.broadcast_to(jnp.exp(lg * lc), (2, 2, lc, V7X_LANES))
    dec = jnp.stack([qd, kd, cd], axis=2)
    dist = pos[:, None] - pos[None, :]
    lgh = log_gamma[:, :, None, None]
    fwd = jnp.where(dist >= 0, jnp.exp(lgh[0] * jnp.maximum(dist, 0.0)), 0.0)
    bwd = jnp.where(dist <= 0, jnp.exp(lgh[1] * jnp.maximum(-dist, 0.0)), 0.0)
    intra = jnp.stack([fwd, bwd]).reshape(2, 2, 2, lc, lc)
    return jnp.swapaxes(dec, 0, 1), jnp.swapaxes(intra, 0, 1)


def kernel(x, c, ctx, c_ctx, w_mod, b_mod, norm1_g, norm2_g, w_in, w_out, da_q_norm_g, da_k_norm_g,
           da_lambda_q1, da_lambda_k1, da_lambda_q2, da_lambda_k2, da_subln_g, ret_decay,
           sg_norm_g, sg_w, sg_b, router_w, ex_w_gate, ex_w_up, ex_w_down):
    b, n, d = x.shape
    n_ctx = ctx.shape[1]
    cos_a, sin_a = _attn_tables(n)
    cos_r, sin_r = _ret_tables(n)
    one_c = jnp.ones((n_ctx, V7X_LANES), F32)
    zero_c = jnp.zeros((n_ctx, V7X_LANES), F32)
    lane = jnp.arange(V7X_LANES)
    seg = (lane[:, None] // 64 == lane[None, :] // 64).astype(BF16)
    state0 = jnp.zeros((b, 2, 2, V7X_LANES, V7X_LANES), F32)
    tk = n_ctx
    assert (n + n_ctx) // tk % ATTN_BLOCKS_PER_STEP == 0
    tok = jnp.arange(MOE_TOKEN_BLOCK)
    tri = (tok[:, None] < tok[None, :]).astype(BF16)
    cap = EC_CAPACITY * n // N_EXPERTS
    cap_c = EC_CAPACITY * n_ctx // N_EXPERTS
    pad, pad_c = min(64, cap), min(64, cap_c)

    for layer in range(DEPTH):
        need_ctx = layer < DEPTH - 1
        lambda_init = 0.8 - 0.6 * math.exp(-0.3 * layer)
        mod = jax.nn.silu(c) @ w_mod[layer] + b_mod[layer]
        mod_c = jnp.broadcast_to(jax.nn.silu(c_ctx) @ w_mod[layer] + b_mod[layer], mod.shape)
        sh1, sc1, g1, sh2, sc2, g2 = jnp.split(mod[:, None, :], 6, axis=-1)
        csh1, csc1, cg1, csh2, csc2, cg2 = jnp.split(mod_c[:, None, :], 6, axis=-1)
        lam = (jnp.exp(jnp.sum(da_lambda_q1[layer] * da_lambda_k1[layer]))
               - jnp.exp(jnp.sum(da_lambda_q2[layer] * da_lambda_k2[layer])) + lambda_init).reshape(1)
        log_gamma = jax.nn.log_sigmoid(ret_decay[layer].astype(F32))
        dec, intra = _ret_decay_tables(log_gamma)
        w_in_b = w_in[layer].astype(BF16)
        w_out_b = w_out[layer].astype(BF16)
        rwt = router_w[layer].T.astype(BF16)
        qg = jnp.tile(da_q_norm_g[layer], 2)[None, :]
        kg = jnp.tile(da_k_norm_g[layer], 2)[None, :]
        gcol = (da_subln_g[layer] * (1.0 - lambda_init))[:, None]
        sg_ng = sg_norm_g[layer][None, :]
        sg_wb = sg_w[layer].astype(BF16)
        sg_bm = jnp.repeat(sg_b[layer].T, SG_GROUP_DIM, axis=1)
        n1, n2 = norm1_g[layer][None, None, :], norm2_g[layer][None, None, :]

        z_lat = in_proj(x, n1 * (1.0 + sc1), sh1, w_in_b, 512)
        z_ctx = in_proj(ctx, n1 * (1.0 + csc1), csh1, w_in_b, n_ctx)

        cq1, cq2, ck, cvt = qkv_prep(z_ctx, one_c, zero_c, qg, kg, seg, n_ctx)
        q1, q2, kl, vlt = qkv_prep(z_lat, cos_a, sin_a, qg, kg, seg, tk)
        k_all = jnp.concatenate([ck, kl], axis=2)
        vt_all = jnp.concatenate([cvt, vlt], axis=2)
        c1, c2 = softmax_shifts(q1, q2, k_all, 256, tk)
        a_lat = diff_attention(lam, q1, q2, c1, c2, k_all, vt_all, gcol, 256, ATTN_BLOCKS_PER_STEP)

        cyf, cyb, cstate = retention(z_ctx, one_c, zero_c, dec, intra, seg, state0, n_ctx)
        yf, yb, _ = retention(z_lat, cos_r, sin_r, dec, intra, seg, cstate, 512)

        s_lat = spatial_gating(z_lat, sg_ng, sg_wb, sg_bm, 512)

        x, f_lat, aff_lat = out_proj(a_lat, yf, yb, s_lat, x, g1, n2 * (1.0 + sc2), sh2, w_out_b, rwt, 512)
        posm, aff5, off = route(aff_lat, tri, cap)
        xs, gs = moe_gather(off, f_lat, posm, aff5, cap, 128)
        xcs = gcs = None
        if need_ctx:
            cc1, cc2 = softmax_shifts(cq1, cq2, ck, n_ctx, tk)
            a_ctx = diff_attention(lam, cq1, cq2, cc1, cc2, ck, cvt, gcol, n_ctx, 1)
            s_ctx = spatial_gating(z_ctx, sg_ng, sg_wb, sg_bm, n_ctx)
            ctx, f_ctx, aff_ctx = out_proj(a_ctx, cyf, cyb, s_ctx, ctx, cg1, n2 * (1.0 + csc2), csh2,
                                           w_out_b, rwt, n_ctx)
            posm_c, aff5_c, off_c = route(aff_ctx, tri, cap_c)
            xcs, gcs = moe_gather(off_c, f_ctx, posm_c, aff5_c, cap_c, cap_c)
        ys, ycs = moe_ffn(jnp.full((1,), layer, jnp.int32), xs, gs, xcs, gcs, ex_w_gate, ex_w_up, ex_w_down,
                          512, pad, pad_c)
        x = moe_scatter(off, x, g2, posm, ys, cap, pad)
        if need_ctx:
            ctx = moe_scatter(off_c, ctx, cg2, posm_c, ycs, cap_c, pad_c)
    return x
```

```python
import functools
import math

import jax
import jax.numpy as jnp
from jax import lax
from jax.experimental import pallas as pl
from jax.experimental.pallas import tpu as pltpu

F32 = jnp.float32
BF16 = jnp.bfloat16

D_MODEL = 1024
DEPTH = 4
GRID_W = 64
NORM_EPS = 1e-6
ROPE_BASE = 10000.0
DA_HEADS = 4
DA_HEAD_DIM = 64
RET_HEADS = 4
RET_QK_DIM = 64
RET_CHUNK = 128
SG_GROUPS = 4
SG_GROUP_DIM = 64
SG_CHUNK = 128
N_EXPERTS = 16
EXPERT_FF = 2 * D_MODEL
EC_CAPACITY = 2
IN_COLS = 3328
COL_AQ, COL_AK, COL_AV = 0, 4, 8
COL_RQ, COL_RK, COL_RV, COL_RGF, COL_RGB = 12, 14, 16, 18, 20
COL_SU256, COL_SV256 = 11, 12

V7X_LANES = 128
V7X_BF16_SUBLANES = 16
V7X_VMEM_BYTES = 64 * 1024 * 1024
MOE_TOKEN_BLOCK = 256
ATTN_BLOCKS_PER_STEP = 11
ATTN_QUERY_TILE = 512
NEG_BIG = -1e30
LOG2_E = 1.4426950408889634
F32_EXP2_RANGE = 120.0


def _cparams(semantics, vmem_mib):
    assert vmem_mib * 1024 * 1024 < V7X_VMEM_BYTES
    return pltpu.CompilerParams(dimension_semantics=semantics, vmem_limit_bytes=vmem_mib * 1024 * 1024)


def _dot(a, b):
    return jnp.dot(a, b, preferred_element_type=F32)


def _dot_nt(a, b):
    return lax.dot_general(a, b, (((1,), (1,)), ((), ())), preferred_element_type=F32)


def _dot_tn(a, b):
    return lax.dot_general(a, b, (((0,), (0,)), ((), ())), preferred_element_type=F32)


def _seg_dot(x, seg):
    hi = x.astype(BF16)
    lo = (x - hi.astype(F32)).astype(BF16)
    return _dot(hi, seg) + _dot(lo, seg)


def _sigmoid(x):
    return 1.0 / (1.0 + jnp.exp(-x))


def _in_proj_body(x_ref, a_ref, s_ref, w_ref, z_ref):
    x = x_ref[...]
    ms = jnp.mean(x * x, axis=-1, keepdims=True)
    h = x * lax.rsqrt(ms + NORM_EPS) * a_ref[...] + s_ref[...]
    z_ref[...] = _dot(h.astype(BF16), w_ref[...])


def in_proj(x, a, s, w, tm):
    b, r, d = x.shape
    c = w.shape[1]
    return pl.pallas_call(
        _in_proj_body,
        grid=(b, r // tm),
        in_specs=[
            pl.BlockSpec((None, tm, d), lambda i, j: (i, j, 0)),
            pl.BlockSpec((None, 1, d), lambda i, j: (i, 0, 0)),
            pl.BlockSpec((None, 1, d), lambda i, j: (i, 0, 0)),
            pl.BlockSpec((d, c), lambda i, j: (0, 0)),
        ],
        out_specs=pl.BlockSpec((None, tm, c), lambda i, j: (i, j, 0)),
        out_shape=jax.ShapeDtypeStruct((b, r, c), F32),
        compiler_params=_cparams(("parallel", "parallel"), 48),
        name="in_proj",
    )(x, a, s, w)


def _qkv_prep_body(q_ref, k_ref, v_ref, cos_ref, sin_ref, qg_ref, kg_ref, seg_ref,
                   qp1_ref, qp2_ref, kk_ref, vt_ref):
    lane = lax.broadcasted_iota(jnp.int32, (1, V7X_LANES), 1)
    lower16 = (lane % 32) < 16
    seg = seg_ref[...]
    cos = cos_ref[...]
    sin = sin_ref[...]

    def norm_rope(x, g):
        ms = _seg_dot(x * x, seg) * (1.0 / DA_HEAD_DIM)
        y = x * lax.rsqrt(ms + NORM_EPS) * g
        partner = jnp.where(lower16, pltpu.roll(y, 112, 1), pltpu.roll(y, 16, 1))
        return y * cos + partner * sin

    q = norm_rope(q_ref[...], qg_ref[...]) * (DA_HEAD_DIM ** -0.5 * LOG2_E)
    first = lane < DA_HEAD_DIM
    qp1_ref[...] = jnp.where(first, q, 0.0).astype(BF16)
    qp2_ref[...] = jnp.where(first, 0.0, q).astype(BF16)
    kk_ref[...] = norm_rope(k_ref[...], kg_ref[...]).astype(BF16)
    tv = vt_ref.shape[2]
    for s in range(vt_ref.shape[0]):
        vt_ref[s] = v_ref[s * tv:(s + 1) * tv, :].T.astype(BF16)


def qkv_prep(z, cos, sin, qg, kg, seg, tm, tv):
    b, r, _ = z.shape
    h = DA_HEADS
    row = lambda off: pl.BlockSpec((None, tm, V7X_LANES), lambda i, j, k: (i, k, off + j))
    tab = pl.BlockSpec((tm, V7X_LANES), lambda i, j, k: (k, 0))
    vec = pl.BlockSpec((1, V7X_LANES), lambda i, j, k: (0, 0))
    o4 = pl.BlockSpec((None, None, tm, V7X_LANES), lambda i, j, k: (i, j, k, 0))
    sd = jax.ShapeDtypeStruct((b, h, r, V7X_LANES), BF16)
    return pl.pallas_call(
        _qkv_prep_body,
        grid=(b, h, r // tm),
        in_specs=[row(COL_AQ), row(COL_AK), row(COL_AV), tab, tab, vec, vec,
                  pl.BlockSpec((V7X_LANES, V7X_LANES), lambda i, j, k: (0, 0))],
        out_specs=[o4, o4, o4,
                   pl.BlockSpec((None, None, tm // tv, V7X_LANES, tv), lambda i, j, k: (i, j, k, 0, 0))],
        out_shape=[sd, sd, sd, jax.ShapeDtypeStruct((b, h, r // tv, V7X_LANES, tv), BF16)],
        compiler_params=_cparams(("parallel", "parallel", "parallel"), 32),
        name="qkv_prep",
    )(z, z, z, cos, sin, qg, kg, seg)


def _attn_body(lam_ref, q1_ref, q2_ref, c1_ref, c2_ref, k_ref, v_ref, g_ref, o_ref, acc_ref, *, n_steps, bps):
    q1 = q1_ref[...]
    q2 = q2_ref[...]
    c1 = c1_ref[...]
    c2 = c2_ref[...]
    tq = q1.shape[0]
    tk = v_ref.shape[2]
    keys = bps * tk

    def probs(i):
        k = k_ref[i * keys:(i + 1) * keys, :]
        p1 = jnp.exp2(_dot_nt(k, q1) - c1)
        p2 = jnp.exp2(_dot_nt(k, q2) - c2)
        return (p1.astype(BF16), p2.astype(BF16),
                jnp.sum(p1.reshape(-1, 8, tq), axis=0), jnp.sum(p2.reshape(-1, 8, tq), axis=0))

    def accumulate(i, p1, p2):
        for idx, p in ((0, p1), (1, p2)):
            part = _dot(v_ref[i * bps], p[0:tk])
            for s in range(1, bps):
                part = part + _dot(v_ref[i * bps + s], p[s * tk:(s + 1) * tk])
            acc_ref[idx] += part

    acc_ref[...] = jnp.zeros(acc_ref.shape, F32)
    p1, p2, l1, l2 = probs(0)
    for i in range(n_steps - 1):
        n1, n2, d1, d2 = probs(i + 1)
        accumulate(i, p1, p2)
        p1, p2, l1, l2 = n1, n2, l1 + d1, l2 + d2
    accumulate(n_steps - 1, p1, p2)
    l1 = jnp.sum(l1, axis=0, keepdims=True)
    l2 = jnp.sum(l2, axis=0, keepdims=True)
    o = acc_ref[0] / l1 - lam_ref[0] * (acc_ref[1] / l2)
    ms = jnp.mean(o * o, axis=0, keepdims=True)
    y = o * lax.rsqrt(ms + NORM_EPS) * g_ref[...]
    o_ref[...] = y.T.astype(BF16)


def diff_attention(lam, q1, q2, c1, c2, k, vt, gcol, tq, bps):
    b, h, nq, _ = q1.shape
    nk = k.shape[2]
    nblk, _, tk = vt.shape[2:]
    qs = pl.BlockSpec((None, None, tq, V7X_LANES), lambda i, j, s: (i, j, s, 0))
    cs = pl.BlockSpec((None, None, 1, tq), lambda i, j, s: (i, j, 0, s))
    return pl.pallas_call(
        functools.partial(_attn_body, n_steps=nblk // bps, bps=bps),
        grid=(b, h, nq // tq),
        in_specs=[pl.BlockSpec(memory_space=pltpu.SMEM), qs, qs, cs, cs,
                  pl.BlockSpec((None, None, nk, V7X_LANES), lambda i, j, s: (i, j, 0, 0)),
                  pl.BlockSpec((None, None, nblk, V7X_LANES, tk), lambda i, j, s: (i, j, 0, 0, 0)),
                  pl.BlockSpec((V7X_LANES, 1), lambda i, j, s: (0, 0))],
        out_specs=pl.BlockSpec((None, tq, V7X_LANES), lambda i, j, s: (i, s, j)),
        out_shape=jax.ShapeDtypeStruct((b, nq, h * V7X_LANES), BF16),
        scratch_shapes=[pltpu.VMEM((2, V7X_LANES, tq), F32)],
        compiler_params=_cparams(("parallel", "parallel", "parallel"), 48),
        name="diff_attention",
    )(lam, q1, q2, c1, c2, k, vt, gcol)


def _rowmax_body(q1_ref, q2_ref, k_ref, m1_ref, m2_ref, *, n_blocks, tk):
    q1 = q1_ref[...]
    q2 = q2_ref[...]

    def body(i, carry):
        k = k_ref[pl.ds(pl.multiple_of(i * tk, tk), tk), :]
        return (jnp.maximum(carry[0], jnp.max(_dot_nt(k, q1), axis=0, keepdims=True)),
                jnp.maximum(carry[1], jnp.max(_dot_nt(k, q2), axis=0, keepdims=True)))

    start = jnp.full((1, q1.shape[0]), NEG_BIG, F32)
    m1, m2 = lax.fori_loop(0, n_blocks, body, (start, start))
    m1_ref[...] = m1
    m2_ref[...] = m2


def score_rowmax(q1, q2, k, tq, tk):
    b, h, nq, _ = q1.shape
    nk = k.shape[2]
    qs = pl.BlockSpec((None, None, tq, V7X_LANES), lambda i, j, s: (i, j, s, 0))
    ms = pl.BlockSpec((None, None, 1, tq), lambda i, j, s: (i, j, 0, s))
    sd = jax.ShapeDtypeStruct((b, h, 1, nq), F32)
    return pl.pallas_call(
        functools.partial(_rowmax_body, n_blocks=nk // tk, tk=tk),
        grid=(b, h, nq // tq),
        in_specs=[qs, qs, pl.BlockSpec((None, None, nk, V7X_LANES), lambda i, j, s: (i, j, 0, 0))],
        out_specs=[ms, ms],
        out_shape=[sd, sd],
        compiler_params=_cparams(("parallel", "parallel", "parallel"), 32),
        name="score_rowmax",
    )(q1, q2, k)


def softmax_shifts(q1, q2, k, tq, tk):
    def sq(t, lo, hi):
        return jnp.sum(jnp.square(t[..., lo:hi].astype(F32)), axis=-1)

    half = DA_HEAD_DIM
    k1 = jnp.max(sq(k, 0, half), axis=-1)
    k2 = jnp.max(sq(k, half, 2 * half), axis=-1)
    b1 = jnp.sqrt(sq(q1, 0, 2 * half) * k1[..., None])[:, :, None, :]
    b2 = jnp.sqrt(sq(q2, 0, 2 * half) * k2[..., None])[:, :, None, :]
    safe = 2.0 * jnp.maximum(jnp.max(b1), jnp.max(b2)) <= F32_EXP2_RANGE
    return lax.cond(safe, lambda: (b1, b2), lambda: tuple(score_rowmax(q1, q2, k, tq, tk)))


def _ret_body(qf_ref, kf_ref, vf_ref, gf_ref, cosf_ref, sinf_ref,
              qb_ref, kb_ref, vb_ref, gb_ref, cosb_ref, sinb_ref,
              dec_ref, intra_ref, seg_ref, s0_ref,
              yf_ref, yb_ref, sout_ref, state_ref, *, n_chunks):
    step = pl.program_id(2)
    lc = RET_CHUNK

    @pl.when(step == 0)
    def _():
        state_ref[...] = s0_ref[...]

    lane = lax.broadcasted_iota(jnp.int32, (1, V7X_LANES), 1)
    lower32 = (lane % RET_QK_DIM) < (RET_QK_DIM // 2)
    head0 = lane < RET_QK_DIM
    blk = (lax.broadcasted_iota(jnp.int32, (V7X_LANES, V7X_LANES), 0) // RET_QK_DIM ==
           lax.broadcasted_iota(jnp.int32, (V7X_LANES, V7X_LANES), 1) // RET_QK_DIM)
    seg = seg_ref[...]

    def rot(x, cos, sin):
        partner = jnp.where(lower32, pltpu.roll(x, 96, 1), pltpu.roll(x, 32, 1))
        return x * cos + partner * sin

    def chunk(d, q_ref, k_ref, v_ref, g_ref, cos_ref, sin_ref, y_ref, c, s):
        rows = pl.ds(c * lc, lc)
        cos = cos_ref[rows, :]
        sin = sin_ref[rows, :]
        q = rot(q_ref[rows, :], cos, sin)
        k = rot(k_ref[rows, :], cos, sin) * (RET_QK_DIM ** -0.5)
        v = v_ref[rows, :]
        qdec = dec_ref[d, 0]
        kdec = dec_ref[d, 1]
        cdec = dec_ref[d, 2][0:1, :]
        kb = k.astype(BF16)
        vb = v.astype(BF16)
        o = _dot(q.astype(BF16), s.astype(BF16)) * qdec
        for hh in range(2):
            sel = head0 if hh == 0 else jnp.logical_not(head0)
            qm = jnp.where(sel, q, 0.0).astype(BF16)
            sc = _dot_nt(qm, kb) * intra_ref[d, hh]
            o = o + _dot(sc.astype(BF16), jnp.where(sel, v, 0.0).astype(BF16))
        kv = _dot_tn((k * kdec).astype(BF16), vb)
        mu = _seg_dot(o, seg) * (1.0 / RET_QK_DIM)
        cen = o - mu
        var = _seg_dot(cen * cen, seg) * (1.0 / RET_QK_DIM)
        g = g_ref[rows, :]
        y_ref[rows, :] = cen * lax.rsqrt(var + NORM_EPS) * (g * _sigmoid(g))
        return s * cdec + jnp.where(blk, kv, 0.0)

    sf = state_ref[0]
    for c in range(n_chunks):
        sf = chunk(0, qf_ref, kf_ref, vf_ref, gf_ref, cosf_ref, sinf_ref, yf_ref, c, sf)
    state_ref[0] = sf
    sb = state_ref[1]
    for c in reversed(range(n_chunks)):
        sb = chunk(1, qb_ref, kb_ref, vb_ref, gb_ref, cosb_ref, sinb_ref, yb_ref, c, sb)
    state_ref[1] = sb

    @pl.when(step == pl.num_programs(2) - 1)
    def _():
        sout_ref[...] = state_ref[...]


def retention(z, cos, sin, dec, intra, seg, s0, tm):
    b, r, _ = z.shape
    nb = r // tm
    fwd = lambda off: pl.BlockSpec((None, tm, V7X_LANES), lambda i, p, k: (i, k, off + p))
    bwd = lambda off: pl.BlockSpec((None, tm, V7X_LANES), lambda i, p, k: (i, nb - 1 - k, off + p))
    tabf = pl.BlockSpec((tm, V7X_LANES), lambda i, p, k: (k, 0))
    tabb = pl.BlockSpec((tm, V7X_LANES), lambda i, p, k: (nb - 1 - k, 0))
    st = pl.BlockSpec((None, None, 2, V7X_LANES, V7X_LANES), lambda i, p, k: (i, p, 0, 0, 0))
    ysd = jax.ShapeDtypeStruct((b, r, 2 * V7X_LANES), F32)
    return pl.pallas_call(
        functools.partial(_ret_body, n_chunks=tm // RET_CHUNK),
        grid=(b, 2, nb),
        in_specs=[fwd(COL_RQ), fwd(COL_RK), fwd(COL_RV), fwd(COL_RGF), tabf, tabf,
                  bwd(COL_RQ), bwd(COL_RK), bwd(COL_RV), bwd(COL_RGB), tabb, tabb,
                  pl.BlockSpec((None, 2, 3, RET_CHUNK, V7X_LANES), lambda i, p, k: (p, 0, 0, 0, 0)),
                  pl.BlockSpec((None, 2, 2, RET_CHUNK, RET_CHUNK), lambda i, p, k: (p, 0, 0, 0, 0)),
                  pl.BlockSpec((V7X_LANES, V7X_LANES), lambda i, p, k: (0, 0)),
                  st],
        out_specs=[pl.BlockSpec((None, tm, V7X_LANES), lambda i, p, k: (i, k, p)),
                   pl.BlockSpec((None, tm, V7X_LANES), lambda i, p, k: (i, nb - 1 - k, p)),
                   st],
        out_shape=[ysd, ysd, jax.ShapeDtypeStruct(s0.shape, F32)],
        scratch_shapes=[pltpu.VMEM((2, V7X_LANES, V7X_LANES), F32)],
        compiler_params=_cparams(("parallel", "parallel", "arbitrary"), 32),
        name="retention",
    )(z, z, z, z, cos, sin, z, z, z, z, cos, sin, dec, intra, seg, s0)


def _sg_body(su_ref, sv_ref, ng_ref, w_ref, bm_ref, o_ref):
    tm = su_ref.shape[0]
    u = jax.nn.gelu(su_ref[...])
    t = jax.nn.gelu(sv_ref[...])
    mu = jnp.mean(t, axis=-1, keepdims=True)
    cen = t - mu
    var = jnp.mean(cen * cen, axis=-1, keepdims=True)
    v = (cen * lax.rsqrt(var + NORM_EPS) * ng_ref[...]).astype(BF16)
    group = lax.broadcasted_iota(jnp.int32, (1, 2 * V7X_LANES), 1) // SG_GROUP_DIM
    bias = bm_ref[...]
    for c in range(tm // SG_CHUNK):
        vc = v[c * SG_CHUNK:(c + 1) * SG_CHUNK, :]
        s = bias
        for g in range(SG_GROUPS):
            s = s + jnp.where(group == g, _dot(w_ref[g], vc), 0.0)
        o_ref[c * SG_CHUNK:(c + 1) * SG_CHUNK, :] = (u[c * SG_CHUNK:(c + 1) * SG_CHUNK, :] * s).astype(BF16)


def spatial_gating(z, ng, w, bm, tm):
    b, r, _ = z.shape
    wide = 2 * V7X_LANES
    return pl.pallas_call(
        _sg_body,
        grid=(b, r // tm),
        in_specs=[pl.BlockSpec((None, tm, wide), lambda i, j: (i, j, COL_SU256)),
                  pl.BlockSpec((None, tm, wide), lambda i, j: (i, j, COL_SV256)),
                  pl.BlockSpec((1, wide), lambda i, j: (0, 0)),
                  pl.BlockSpec((SG_GROUPS, SG_CHUNK, SG_CHUNK), lambda i, j: (0, 0, 0)),
                  pl.BlockSpec((SG_CHUNK, wide), lambda i, j: (0, 0))],
        out_specs=pl.BlockSpec((None, tm, wide), lambda i, j: (i, j, 0)),
        out_shape=jax.ShapeDtypeStruct((b, r, wide), BF16),
        compiler_params=_cparams(("parallel", "parallel"), 32),
        name="spatial_gating",
    )(z, z, ng, w, bm)


def _out_proj_body(a_ref, yf_ref, yb_ref, sg_ref, x_ref, g1_ref, a2_ref, s2_ref, w_ref, rwt_ref,
                   xo_ref, f_ref, aff_ref):
    na = a_ref.shape[1]
    nr = yf_ref.shape[1]
    r = (yf_ref[...] + yb_ref[...]).astype(BF16)
    y = (_dot(a_ref[...], w_ref[0:na, :]) + _dot(r, w_ref[na:na + nr, :])
         + _dot(sg_ref[...], w_ref[na + nr:, :]))
    xn = x_ref[...] + g1_ref[...] * y
    xo_ref[...] = xn
    ms = jnp.mean(xn * xn, axis=-1, keepdims=True)
    f = (xn * lax.rsqrt(ms + NORM_EPS) * a2_ref[...] + s2_ref[...]).astype(BF16)
    f_ref[...] = f
    logits = _dot_nt(rwt_ref[...], f)
    e = jnp.exp(logits - jnp.max(logits, axis=0, keepdims=True))
    aff_ref[...] = e / jnp.sum(e, axis=0, keepdims=True)


def out_proj(a, yf, yb, sg, x, g1, a2, s2, w, rwt, tm):
    b, r, d = x.shape
    rows = lambda width: pl.BlockSpec((None, tm, width), lambda i, j: (i, j, 0))
    vec = pl.BlockSpec((None, 1, d), lambda i, j: (i, 0, 0))
    return pl.pallas_call(
        _out_proj_body,
        grid=(b, r // tm),
        in_specs=[rows(a.shape[2]), rows(yf.shape[2]), rows(yb.shape[2]), rows(sg.shape[2]), rows(d),
                  vec, vec, vec,
                  pl.BlockSpec(w.shape, lambda i, j: (0, 0)),
                  pl.BlockSpec(rwt.shape, lambda i, j: (0, 0))],
        out_specs=[rows(d), rows(d), pl.BlockSpec((None, N_EXPERTS, tm), lambda i, j: (i, 0, j))],
        out_shape=[jax.ShapeDtypeStruct((b, r, d), F32), jax.ShapeDtypeStruct((b, r, d), BF16),
                   jax.ShapeDtypeStruct((b, N_EXPERTS, r), F32)],
        compiler_params=_cparams(("parallel", "parallel"), 32),
        name="out_proj",
    )(a, yf, yb, sg, x, g1, a2, s2, w, rwt)


def _router_body(aff_ref, tri_ref, posm_ref, off_ref, *, cap, tb):
    aff = aff_ref[...]
    e, r = aff.shape
    nb = r // tb
    bits = pltpu.bitcast(aff, jnp.int32)

    def search(i, thr):
        cand = thr | lax.shift_left(jnp.int32(1), 30 - i)
        cnt = jnp.sum(jnp.where(bits >= cand, 1.0, 0.0), axis=1, keepdims=True)
        return jnp.where(cnt >= cap, cand, thr)

    thr = lax.fori_loop(0, 31, search, jnp.zeros((e, 1), jnp.int32))
    gt = jnp.where(bits > thr, 1.0, 0.0)
    eq = jnp.where(bits == thr, 1.0, 0.0)
    need = cap - jnp.sum(gt, axis=1, keepdims=True)
    tri = tri_ref[...]
    lane = lax.broadcasted_iota(jnp.int32, (e, V7X_LANES), 1)
    offs = jnp.zeros((e, V7X_LANES), jnp.int32)
    run_eq = jnp.zeros((e, 1), F32)
    run_sel = jnp.zeros((e, 1), F32)
    for j in range(nb):
        cols = slice(j * tb, (j + 1) * tb)
        eq_j = eq[:, cols]
        before_eq = _dot(eq_j.astype(BF16), tri) + run_eq
        sel_j = gt[:, cols] + eq_j * jnp.where(before_eq < need, 1.0, 0.0)
        pos_j = _dot(sel_j.astype(BF16), tri) + run_sel
        posm_ref[:, cols] = jnp.where(sel_j > 0.0, pos_j, -1.0)
        offs = jnp.where(lane == j, run_sel.astype(jnp.int32), offs)
        run_eq = run_eq + jnp.sum(eq_j, axis=1, keepdims=True)
        run_sel = run_sel + jnp.sum(sel_j, axis=1, keepdims=True)
    off_ref[...] = jnp.where(lane == nb, run_sel.astype(jnp.int32), offs)


def route(aff_t, tri, cap):
    b, e, r = aff_t.shape
    tb = tri.shape[0]
    posm, off = pl.pallas_call(
        functools.partial(_router_body, cap=cap, tb=tb),
        grid=(b,),
        in_specs=[pl.BlockSpec((None, e, r), lambda i: (i, 0, 0)),
                  pl.BlockSpec((tb, tb), lambda i: (0, 0))],
        out_specs=[pl.BlockSpec((None, e, r), lambda i: (i, 0, 0)),
                   pl.BlockSpec((None, e, V7X_LANES), lambda i: (i, 0, 0))],
        out_shape=[jax.ShapeDtypeStruct((b, e, r), F32), jax.ShapeDtypeStruct((b, e, V7X_LANES), jnp.int32)],
        compiler_params=_cparams(("parallel",), 32),
        name="router",
    )(aff_t, tri)
    shape5 = (b, e, r // tb, 1, tb)
    return posm.reshape(shape5), aff_t.reshape(shape5), off.reshape(-1)


def _gather_body(off_ref, f_hbm, posm_ref, aff_ref, x_ref, g_ref, fbuf, xbuf, gbuf, sem, *, cap, w, tb, nb, unroll):
    b = pl.program_id(0)
    e = pl.program_id(1)

    @pl.when(e == 0)
    def _():
        cp = pltpu.make_async_copy(f_hbm.at[b], fbuf, sem)
        cp.start()
        cp.wait()

    base = (b * pl.num_programs(1) + e) * V7X_LANES
    iota = lax.broadcasted_iota(jnp.int32, (w, 1), 0).astype(F32)
    align = 8
    xbuf[...] = jnp.zeros(xbuf.shape, F32)
    gbuf[...] = jnp.zeros(gbuf.shape, F32)

    def window(j, rnd):
        return pl.multiple_of((off_ref[base + j] // align) * align + rnd * w, align)

    def rows_of(j, rnd):
        onehot = posm_ref[j] == window(j, rnd).astype(F32) + iota
        toks = fbuf[pl.ds(pl.multiple_of(j * tb, tb), tb), :]
        return (_dot(jnp.where(onehot, 1.0, 0.0).astype(BF16), toks),
                jnp.sum(jnp.where(onehot, aff_ref[j], 0.0), axis=1, keepdims=True))

    def add(j, rnd, vals):
        rows = pl.ds(window(j, rnd), w)
        xbuf[rows, :] += vals[0]
        gbuf[rows, :] += vals[1]

    def first_round(i, carry):
        js = [i * unroll + u for u in range(unroll)]
        vals = [rows_of(j, 0) for j in js]
        for j, v in zip(js, vals):
            add(j, 0, v)
        return carry

    lax.fori_loop(0, nb // unroll, first_round, 0)

    def more_rounds(j, carry):
        span = off_ref[base + j + 1] - (off_ref[base + j] // align) * align

        def one(rnd, c):
            add(j, rnd, rows_of(j, rnd))
            return c

        return lax.fori_loop(1, (span + w - 1) // w, one, carry)

    lax.fori_loop(0, nb, more_rounds, 0)
    x_ref[...] = xbuf[0:cap, :].astype(BF16)
    g_ref[...] = gbuf[0:cap, :]


def moe_gather(off, f, posm, aff, cap, w):
    b, r, d = f.shape
    _, e, nb, _, tb = posm.shape
    unroll = math.gcd(4, nb)
    blk = pl.BlockSpec((None, None, nb, 1, tb), lambda i, j, o: (i, j, 0, 0, 0))
    return pl.pallas_call(
        functools.partial(_gather_body, cap=cap, w=w, tb=tb, nb=nb, unroll=unroll),
        grid_spec=pltpu.PrefetchScalarGridSpec(
            num_scalar_prefetch=1,
            grid=(b, e),
            in_specs=[pl.BlockSpec(memory_space=pl.ANY), blk, blk],
            out_specs=[pl.BlockSpec((None, None, cap, d), lambda i, j, o: (i, j, 0, 0)),
                       pl.BlockSpec((None, None, cap, 1), lambda i, j, o: (i, j, 0, 0))],
            scratch_shapes=[pltpu.VMEM((r, d), BF16), pltpu.VMEM((cap + w, d), F32), pltpu.VMEM((cap + w, 1), F32),
                            pltpu.SemaphoreType.DMA(())]),
        out_shape=[jax.ShapeDtypeStruct((b, e, cap, d), BF16), jax.ShapeDtypeStruct((b, e, cap, 1), F32)],
        compiler_params=_cparams(("arbitrary", "arbitrary"), 40),
        name="moe_gather",
    )(off, f, posm, aff)


def _moe_ffn_body(*refs, has_ctx, cap, capc):
    if has_ctx:
        _, x_ref, g_ref, xc_ref, gc_ref, wg_ref, wu_ref, wd_ref, y_ref, yc_ref, yacc, ycacc = refs
    else:
        _, x_ref, g_ref, wg_ref, wu_ref, wd_ref, y_ref, yacc = refs
    c = pl.program_id(1)
    last = pl.num_programs(1) - 1
    nbatch, _, d = x_ref.shape
    wg = wg_ref[...].astype(BF16)
    wu = wu_ref[...].astype(BF16)
    wd = wd_ref[...].astype(BF16)

    def swiglu(x):
        a = _dot(x, wg)
        u = _dot(x, wu)
        return _dot((a * _sigmoid(a) * u).astype(BF16), wd)

    def accumulate(acc, idx, part):
        @pl.when(c == 0)
        def _():
            acc[idx] = part

        @pl.when(c > 0)
        def _():
            acc[idx] += part

    for i in range(nbatch):
        accumulate(yacc, i, swiglu(x_ref[i]))
    if has_ctx:
        accumulate(ycacc, slice(None), swiglu(xc_ref[...].reshape(nbatch * capc, d)))

    @pl.when(c == last)
    def _():
        for i in range(nbatch):
            y_ref[i, 0:cap, :] = (yacc[i] * g_ref[i]).astype(BF16)
            y_ref[i, cap:, :] = jnp.zeros((y_ref.shape[1] - cap, d), BF16)
        if has_ctx:
            yc = ycacc[...] * gc_ref[...].reshape(nbatch * capc, 1)
            yc_ref[:, 0:capc, :] = yc.reshape(nbatch, capc, d).astype(BF16)
            yc_ref[:, capc:, :] = jnp.zeros((nbatch, yc_ref.shape[1] - capc, d), BF16)


def moe_ffn(layer, x, g, xc, gc, wg, wu, wd, tf, pad, padc):
    b, e, cap, d = x.shape
    ff = wg.shape[3]
    has_ctx = xc is not None
    capc = xc.shape[2] if has_ctx else 0
    tok = lambda rows, width: pl.BlockSpec((b, None, rows, width), lambda j, k, l: (0, j, 0, 0))
    in_specs = [tok(cap, d), tok(cap, 1)]
    out_specs = [tok(cap + pad, d)]
    out_shape = [jax.ShapeDtypeStruct((b, e, cap + pad, d), BF16)]
    scratch = [pltpu.VMEM((b, cap, d), F32)]
    args = [x, g]
    if has_ctx:
        in_specs += [tok(capc, d), tok(capc, 1)]
        out_specs.append(tok(capc + padc, d))
        out_shape.append(jax.ShapeDtypeStruct((b, e, capc + padc, d), BF16))
        scratch.append(pltpu.VMEM((b * capc, d), F32))
        args += [xc, gc]
    in_specs += [pl.BlockSpec((None, None, d, tf), lambda j, k, l: (l[0], j, 0, k)),
                 pl.BlockSpec((None, None, d, tf), lambda j, k, l: (l[0], j, 0, k)),
                 pl.BlockSpec((None, None, tf, d), lambda j, k, l: (l[0], j, k, 0))]
    out = pl.pallas_call(
        functools.partial(_moe_ffn_body, has_ctx=has_ctx, cap=cap, capc=capc),
        grid_spec=pltpu.PrefetchScalarGridSpec(
            num_scalar_prefetch=1, grid=(e, ff // tf),
            in_specs=in_specs, out_specs=out_specs, scratch_shapes=scratch),
        out_shape=out_shape,
        compiler_params=_cparams(("parallel", "arbitrary"), 56),
        name="moe_ffn",
    )(layer, *args, wg, wu, wd)
    return out if has_ctx else (out[0], None)


def _scatter_body(off_ref, x_ref, g2_ref, posm_ref, y_hbm, xo_ref, ybuf, pcat, sem, *, cap, w, nb, n_exp):
    b = pl.program_id(0)
    j = pl.program_id(1)
    step = b * nb + j
    slot = lax.rem(step, 2)
    spare = 2
    align = V7X_BF16_SUBLANES

    def window(bb, jj, e, rnd):
        off = off_ref[(bb * n_exp + e) * V7X_LANES + jj]
        return jnp.minimum((off // align) * align + rnd * w, cap)

    def copies(bb, jj, buf, rnd):
        return [pltpu.make_async_copy(
            y_hbm.at[bb, e, pl.ds(pl.multiple_of(window(bb, jj, e, rnd), align), w), :],
            ybuf.at[buf, pl.ds(e * w, w), :], sem.at[buf]) for e in range(n_exp)]

    @pl.when(step == 0)
    def _():
        for cp in copies(b, j, 0, 0):
            cp.start()

    @pl.when(step + 1 < pl.num_programs(0) * nb)
    def _():
        nxt = step + 1
        for cp in copies(nxt // nb, lax.rem(nxt, nb), 1 - slot, 0):
            cp.start()

    iota = lax.broadcasted_iota(jnp.int32, (w, 1), 0).astype(F32)

    def onehots(rnd):
        for e in range(n_exp):
            first = window(b, j, e, rnd).astype(F32)
            pcat[e * w:(e + 1) * w, :] = jnp.where(posm_ref[e] == first + iota, 1.0, 0.0).astype(BF16)

    for cp in copies(b, j, slot, 0):
        cp.wait()
    onehots(0)
    g2 = g2_ref[...]
    xo_ref[...] = x_ref[...] + g2 * _dot_tn(pcat[...], ybuf[slot])

    span = jnp.int32(0)
    for e in range(n_exp):
        idx = (b * n_exp + e) * V7X_LANES + j
        span = jnp.maximum(span, off_ref[idx + 1] - (off_ref[idx] // align) * align)

    def extra(rnd, carry):
        for cp in copies(b, j, spare, rnd):
            cp.start()
        for cp in copies(b, j, spare, rnd):
            cp.wait()
        onehots(rnd)
        xo_ref[...] += g2 * _dot_tn(pcat[...], ybuf[spare])
        return carry

    lax.fori_loop(1, (span + w - 1) // w, extra, 0)


def moe_scatter(off, x, g2, posm, y, cap, w):
    b, r, d = x.shape
    _, e, nb, _, tb = posm.shape
    rows = pl.BlockSpec((None, tb, d), lambda i, j, o: (i, j, 0))
    return pl.pallas_call(
        functools.partial(_scatter_body, cap=cap, w=w, nb=nb, n_exp=e),
        grid_spec=pltpu.PrefetchScalarGridSpec(
            num_scalar_prefetch=1,
            grid=(b, nb),
            in_specs=[rows, pl.BlockSpec((None, 1, d), lambda i, j, o: (i, 0, 0)),
                      pl.BlockSpec((None, e, None, 1, tb), lambda i, j, o: (i, 0, j, 0, 0)),
                      pl.BlockSpec(memory_space=pl.ANY)],
            out_specs=rows,
            scratch_shapes=[pltpu.VMEM((3, e * w, d), BF16), pltpu.VMEM((e * w, tb), BF16),
                            pltpu.SemaphoreType.DMA((3,))]),
        out_shape=jax.ShapeDtypeStruct((b, r, d), F32),
        compiler_params=_cparams(("arbitrary", "arbitrary"), 32),
        name="moe_scatter",
    )(off, x, g2, posm, y)


def _attn_tables(n):
    rows = n // GRID_W
    row = jnp.broadcast_to(jnp.arange(rows, dtype=F32)[:, None], (rows, GRID_W)).reshape(-1)
    col = jnp.broadcast_to(jnp.arange(GRID_W, dtype=F32)[None, :], (rows, GRID_W)).reshape(-1)
    quarter = DA_HEAD_DIM // 4
    inv = ROPE_BASE ** (-jnp.arange(quarter, dtype=F32) / quarter)
    ang_r, ang_c = row[:, None] * inv, col[:, None] * inv

    def half(ang):
        return (jnp.concatenate([jnp.cos(ang), jnp.cos(ang)], -1),
                jnp.concatenate([-jnp.sin(ang), jnp.sin(ang)], -1))

    cr, sr = half(ang_r)
    cc, sc = half(ang_c)
    cos64 = jnp.concatenate([cr, cc], -1)
    sin64 = jnp.concatenate([sr, sc], -1)
    return jnp.tile(cos64, (1, 2)), jnp.tile(sin64, (1, 2))


def _ret_tables(n):
    inv = 1.0 / (ROPE_BASE ** jnp.linspace(0.0, 1.0, RET_QK_DIM // 2, dtype=F32))
    ang = jnp.arange(n, dtype=F32)[:, None] * inv
    cos64 = jnp.concatenate([jnp.cos(ang), jnp.cos(ang)], -1)
    sin64 = jnp.concatenate([-jnp.sin(ang), jnp.sin(ang)], -1)
    return jnp.tile(cos64, (1, 2)), jnp.tile(sin64, (1, 2))


def _ret_decay_tables(log_gamma):
    lc = RET_CHUNK
    pos = jnp.arange(lc, dtype=F32)
    lg = jnp.repeat(log_gamma, RET_QK_DIM, axis=1).reshape(2, 2, 1, V7X_LANES)
    qd = jnp.stack([jnp.exp(lg[0] * (pos + 1.0)[None, :, None]), jnp.exp(lg[1] * (lc - pos)[None, :, None])])
    kd = jnp.stack([jnp.exp(lg[0] * (lc - 1.0 - pos)[None, :, None]), jnp.exp(lg[1] * pos[None, :, None])])
    cd = jnp.broadcast_to(jnp.exp(lg * lc), (2, 2, lc, V7X_LANES))
    dec = jnp.stack([qd, kd, cd], axis=2)
    dist = pos[:, None] - pos[None, :]
    lgh = log_gamma[:, :, None, None]
    fwd = jnp.where(dist >= 0, jnp.exp(lgh[0] * jnp.maximum(dist, 0.0)), 0.0)
    bwd = jnp.where(dist <= 0, jnp.exp(lgh[1] * jnp.maximum(-dist, 0.0)), 0.0)
    intra = jnp.stack([fwd, bwd]).reshape(2, 2, 2, lc, lc)
    return jnp.swapaxes(dec, 0, 1), jnp.swapaxes(intra, 0, 1)


def kernel(x, c, ctx, c_ctx, w_mod, b_mod, norm1_g, norm2_g, w_in, w_out, da_q_norm_g, da_k_norm_g,
           da_lambda_q1, da_lambda_k1, da_lambda_q2, da_lambda_k2, da_subln_g, ret_decay,
           sg_norm_g, sg_w, sg_b, router_w, ex_w_gate, ex_w_up, ex_w_down):
    b, n, d = x.shape
    n_ctx = ctx.shape[1]
    cos_a, sin_a = _attn_tables(n)
    cos_r, sin_r = _ret_tables(n)
    one_c = jnp.ones((n_ctx, V7X_LANES), F32)
    zero_c = jnp.zeros((n_ctx, V7X_LANES), F32)
    lane = jnp.arange(V7X_LANES)
    seg = (lane[:, None] // 64 == lane[None, :] // 64).astype(BF16)
    state0 = jnp.zeros((b, 2, 2, V7X_LANES, V7X_LANES), F32)
    tk = n_ctx
    assert (n + n_ctx) // tk % ATTN_BLOCKS_PER_STEP == 0
    tok = jnp.arange(MOE_TOKEN_BLOCK)
    tri = (tok[:, None] < tok[None, :]).astype(BF16)
    cap = EC_CAPACITY * n // N_EXPERTS
    cap_c = EC_CAPACITY * n_ctx // N_EXPERTS
    pad, pad_c = min(64, cap), min(64, cap_c)

    for layer in range(DEPTH):
        need_ctx = layer < DEPTH - 1
        lambda_init = 0.8 - 0.6 * math.exp(-0.3 * layer)
        mod = jax.nn.silu(c) @ w_mod[layer] + b_mod[layer]
        mod_c = jnp.broadcast_to(jax.nn.silu(c_ctx) @ w_mod[layer] + b_mod[layer], mod.shape)
        sh1, sc1, g1, sh2, sc2, g2 = jnp.split(mod[:, None, :], 6, axis=-1)
        csh1, csc1, cg1, csh2, csc2, cg2 = jnp.split(mod_c[:, None, :], 6, axis=-1)
        lam = (jnp.exp(jnp.sum(da_lambda_q1[layer] * da_lambda_k1[layer]))
               - jnp.exp(jnp.sum(da_lambda_q2[layer] * da_lambda_k2[layer])) + lambda_init).reshape(1)
        log_gamma = jax.nn.log_sigmoid(ret_decay[layer].astype(F32))
        dec, intra = _ret_decay_tables(log_gamma)
        w_in_b = w_in[layer].astype(BF16)
        w_out_b = w_out[layer].astype(BF16)
        rwt = router_w[layer].T.astype(BF16)
        qg = jnp.tile(da_q_norm_g[layer], 2)[None, :]
        kg = jnp.tile(da_k_norm_g[layer], 2)[None, :]
        gcol = (da_subln_g[layer] * (1.0 - lambda_init))[:, None]
        sg_ng = sg_norm_g[layer][None, :]
        sg_wb = sg_w[layer].astype(BF16)
        sg_bm = jnp.repeat(sg_b[layer].T, SG_GROUP_DIM, axis=1)
        n1, n2 = norm1_g[layer][None, None, :], norm2_g[layer][None, None, :]

        z_lat = in_proj(x, n1 * (1.0 + sc1), sh1, w_in_b, 512)
        z_ctx = in_proj(ctx, n1 * (1.0 + csc1), csh1, w_in_b, n_ctx)

        cq1, cq2, ck, cvt = qkv_prep(z_ctx, one_c, zero_c, qg, kg, seg, n_ctx, tk)
        q1, q2, kl, vlt = qkv_prep(z_lat, cos_a, sin_a, qg, kg, seg, 512, tk)
        k_all = jnp.concatenate([ck, kl], axis=2)
        vt_all = jnp.concatenate([cvt, vlt], axis=2)
        c1, c2 = softmax_shifts(q1, q2, k_all, ATTN_QUERY_TILE, tk)
        a_lat = diff_attention(lam, q1, q2, c1, c2, k_all, vt_all, gcol, ATTN_QUERY_TILE, ATTN_BLOCKS_PER_STEP)

        cyf, cyb, cstate = retention(z_ctx, one_c, zero_c, dec, intra, seg, state0, n_ctx)
        yf, yb, _ = retention(z_lat, cos_r, sin_r, dec, intra, seg, cstate, 512)

        s_lat = spatial_gating(z_lat, sg_ng, sg_wb, sg_bm, 512)

        x, f_lat, aff_lat = out_proj(a_lat, yf, yb, s_lat, x, g1, n2 * (1.0 + sc2), sh2, w_out_b, rwt, 512)
        posm, aff5, off = route(aff_lat, tri, cap)
        xs, gs = moe_gather(off, f_lat, posm, aff5, cap, pad)
        xcs = gcs = None
        if need_ctx:
            cc1, cc2 = softmax_shifts(cq1, cq2, ck, n_ctx, tk)
            a_ctx = diff_attention(lam, cq1, cq2, cc1, cc2, ck, cvt, gcol, n_ctx, 1)
            s_ctx = spatial_gating(z_ctx, sg_ng, sg_wb, sg_bm, n_ctx)
            ctx, f_ctx, aff_ctx = out_proj(a_ctx, cyf, cyb, s_ctx, ctx, cg1, n2 * (1.0 + csc2), csh2,
                                           w_out_b, rwt, n_ctx)
            posm_c, aff5_c, off_c = route(aff_ctx, tri, cap_c)
            xcs, gcs = moe_gather(off_c, f_ctx, posm_c, aff5_c, cap_c, pad_c)
        ys, ycs = moe_ffn(jnp.full((1,), layer, jnp.int32), xs, gs, xcs, gcs, ex_w_gate, ex_w_up, ex_w_down,
                          512, pad, pad_c)
        x = moe_scatter(off, x, g2, posm, ys, cap, pad)
        if need_ctx:
            ctx = moe_scatter(off_c, ctx, cg2, posm_c, ycs, cap_c, pad_c)
    return x
```

```python
import functools
import math

import jax
import jax.numpy as jnp
from jax import lax
from jax.experimental import pallas as pl
from jax.experimental.pallas import tpu as pltpu

F32 = jnp.float32
BF16 = jnp.bfloat16

D_MODEL = 1024
DEPTH = 4
GRID_W = 64
NORM_EPS = 1e-6
ROPE_BASE = 10000.0
DA_HEADS = 4
DA_HEAD_DIM = 64
RET_HEADS = 4
RET_QK_DIM = 64
RET_SCAN_CHUNK = 128
SG_GROUPS = 4
SG_GROUP_DIM = 64
SG_CHUNK = 128
N_EXPERTS = 16
EXPERT_FF = 2 * D_MODEL
EC_CAPACITY = 2
IN_COLS = 3328
COL_AQ, COL_AK, COL_AV = 0, 4, 8
COL_RQ, COL_RK, COL_RV, COL_RGF, COL_RGB = 12, 14, 16, 18, 20
COL_SU256, COL_SV256 = 11, 12

V7X_LANES = 128
V7X_BF16_SUBLANES = 16
V7X_VMEM_BYTES = 64 * 1024 * 1024
MOE_TOKEN_BLOCK = 256
ATTN_BLOCKS_PER_STEP = 11
ATTN_QUERY_TILE = 512
NEG_BIG = -1e30
LOG2_E = 1.4426950408889634
F32_EXP2_RANGE = 120.0


def _cparams(semantics, vmem_mib):
    assert vmem_mib * 1024 * 1024 < V7X_VMEM_BYTES
    return pltpu.CompilerParams(dimension_semantics=semantics, vmem_limit_bytes=vmem_mib * 1024 * 1024)


def _row_tile(rows, preferred):
    while rows % preferred:
        preferred //= 2
    return preferred


def _dot(a, b):
    return jnp.dot(a, b, preferred_element_type=F32)


def _dot_nt(a, b):
    return lax.dot_general(a, b, (((1,), (1,)), ((), ())), preferred_element_type=F32)


def _dot_tn(a, b):
    return lax.dot_general(a, b, (((0,), (0,)), ((), ())), preferred_element_type=F32)


def _seg_dot(x, seg):
    hi = x.astype(BF16)
    lo = (x - hi.astype(F32)).astype(BF16)
    return _dot(hi, seg) + _dot(lo, seg)


def _sigmoid(x):
    return 1.0 / (1.0 + jnp.exp(-x))


def _in_proj_body(x_ref, a_ref, s_ref, w_ref, z_ref):
    x = x_ref[...]
    ms = jnp.mean(x * x, axis=-1, keepdims=True)
    h = x * lax.rsqrt(ms + NORM_EPS) * a_ref[...] + s_ref[...]
    z_ref[...] = _dot(h.astype(BF16), w_ref[...])


def in_proj(x, a, s, w, tm):
    b, r, d = x.shape
    c = w.shape[1]
    return pl.pallas_call(
        _in_proj_body,
        grid=(b, r // tm),
        in_specs=[
            pl.BlockSpec((None, tm, d), lambda i, j: (i, j, 0)),
            pl.BlockSpec((None, 1, d), lambda i, j: (i, 0, 0)),
            pl.BlockSpec((None, 1, d), lambda i, j: (i, 0, 0)),
            pl.BlockSpec((d, c), lambda i, j: (0, 0)),
        ],
        out_specs=pl.BlockSpec((None, tm, c), lambda i, j: (i, j, 0)),
        out_shape=jax.ShapeDtypeStruct((b, r, c), F32),
        compiler_params=_cparams(("parallel", "parallel"), 48),
        name="in_proj",
    )(x, a, s, w)


def _qkv_prep_body(q_ref, k_ref, v_ref, cos_ref, sin_ref, qg_ref, kg_ref, seg_ref,
                   qp1_ref, qp2_ref, kk_ref, vt_ref):
    lane = lax.broadcasted_iota(jnp.int32, (1, V7X_LANES), 1)
    lower16 = (lane % 32) < 16
    seg = seg_ref[...]
    cos = cos_ref[...]
    sin = sin_ref[...]

    def norm_rope(x, g):
        ms = _seg_dot(x * x, seg) * (1.0 / DA_HEAD_DIM)
        y = x * lax.rsqrt(ms + NORM_EPS) * g
        partner = jnp.where(lower16, pltpu.roll(y, 112, 1), pltpu.roll(y, 16, 1))
        return y * cos + partner * sin

    q = norm_rope(q_ref[...], qg_ref[...]) * (DA_HEAD_DIM ** -0.5 * LOG2_E)
    first = lane < DA_HEAD_DIM
    qp1_ref[...] = jnp.where(first, q, 0.0).astype(BF16)
    qp2_ref[...] = jnp.where(first, 0.0, q).astype(BF16)
    kk_ref[...] = norm_rope(k_ref[...], kg_ref[...]).astype(BF16)
    tv = vt_ref.shape[2]
    for s in range(vt_ref.shape[0]):
        vt_ref[s] = v_ref[s * tv:(s + 1) * tv, :].T.astype(BF16)


def qkv_prep(z, cos, sin, qg, kg, seg, tm, tv):
    b, r, _ = z.shape
    h = DA_HEADS
    row = lambda off: pl.BlockSpec((None, tm, V7X_LANES), lambda i, j, k: (i, k, off + j))
    tab = pl.BlockSpec((tm, V7X_LANES), lambda i, j, k: (k, 0))
    vec = pl.BlockSpec((1, V7X_LANES), lambda i, j, k: (0, 0))
    o4 = pl.BlockSpec((None, None, tm, V7X_LANES), lambda i, j, k: (i, j, k, 0))
    sd = jax.ShapeDtypeStruct((b, h, r, V7X_LANES), BF16)
    return pl.pallas_call(
        _qkv_prep_body,
        grid=(b, h, r // tm),
        in_specs=[row(COL_AQ), row(COL_AK), row(COL_AV), tab, tab, vec, vec,
                  pl.BlockSpec((V7X_LANES, V7X_LANES), lambda i, j, k: (0, 0))],
        out_specs=[o4, o4, o4,
                   pl.BlockSpec((None, None, tm // tv, V7X_LANES, tv), lambda i, j, k: (i, j, k, 0, 0))],
        out_shape=[sd, sd, sd, jax.ShapeDtypeStruct((b, h, r // tv, V7X_LANES, tv), BF16)],
        compiler_params=_cparams(("parallel", "parallel", "parallel"), 32),
        name="qkv_prep",
    )(z, z, z, cos, sin, qg, kg, seg)


def _attn_body(lam_ref, q1_ref, q2_ref, c1_ref, c2_ref, k_ref, v_ref, g_ref, o_ref, acc_ref, *, n_steps, bps):
    q1 = q1_ref[...]
    q2 = q2_ref[...]
    c1 = c1_ref[...]
    c2 = c2_ref[...]
    tq = q1.shape[0]
    tk = v_ref.shape[2]
    keys = bps * tk

    def probs(i):
        k = k_ref[i * keys:(i + 1) * keys, :]
        p1 = jnp.exp2(_dot_nt(k, q1) - c1)
        p2 = jnp.exp2(_dot_nt(k, q2) - c2)
        return (p1.astype(BF16), p2.astype(BF16),
                jnp.sum(p1.reshape(-1, 8, tq), axis=0), jnp.sum(p2.reshape(-1, 8, tq), axis=0))

    def accumulate(i, p1, p2):
        for idx, p in ((0, p1), (1, p2)):
            part = _dot(v_ref[i * bps], p[0:tk])
            for s in range(1, bps):
                part = part + _dot(v_ref[i * bps + s], p[s * tk:(s + 1) * tk])
            acc_ref[idx] += part

    acc_ref[...] = jnp.zeros(acc_ref.shape, F32)
    p1, p2, l1, l2 = probs(0)
    for i in range(n_steps - 1):
        n1, n2, d1, d2 = probs(i + 1)
        accumulate(i, p1, p2)
        p1, p2, l1, l2 = n1, n2, l1 + d1, l2 + d2
    accumulate(n_steps - 1, p1, p2)
    l1 = jnp.sum(l1, axis=0, keepdims=True)
    l2 = jnp.sum(l2, axis=0, keepdims=True)
    o = acc_ref[0] / l1 - lam_ref[0] * (acc_ref[1] / l2)
    ms = jnp.mean(o * o, axis=0, keepdims=True)
    y = o * lax.rsqrt(ms + NORM_EPS) * g_ref[...]
    o_ref[...] = y.T.astype(BF16)


def diff_attention(lam, q1, q2, c1, c2, k, vt, gcol, tq, bps):
    b, h, nq, _ = q1.shape
    nk = k.shape[2]
    nblk, _, tk = vt.shape[2:]
    qs = pl.BlockSpec((None, None, tq, V7X_LANES), lambda i, j, s: (i, j, s, 0))
    cs = pl.BlockSpec((None, None, 1, tq), lambda i, j, s: (i, j, 0, s))
    return pl.pallas_call(
        functools.partial(_attn_body, n_steps=nblk // bps, bps=bps),
        grid=(b, h, nq // tq),
        in_specs=[pl.BlockSpec(memory_space=pltpu.SMEM), qs, qs, cs, cs,
                  pl.BlockSpec((None, None, nk, V7X_LANES), lambda i, j, s: (i, j, 0, 0)),
                  pl.BlockSpec((None, None, nblk, V7X_LANES, tk), lambda i, j, s: (i, j, 0, 0, 0)),
                  pl.BlockSpec((V7X_LANES, 1), lambda i, j, s: (0, 0))],
        out_specs=pl.BlockSpec((None, tq, V7X_LANES), lambda i, j, s: (i, s, j)),
        out_shape=jax.ShapeDtypeStruct((b, nq, h * V7X_LANES), BF16),
        scratch_shapes=[pltpu.VMEM((2, V7X_LANES, tq), F32)],
        compiler_params=_cparams(("parallel", "parallel", "parallel"), 48),
        name="diff_attention",
    )(lam, q1, q2, c1, c2, k, vt, gcol)


def _rowmax_body(q1_ref, q2_ref, k_ref, m1_ref, m2_ref, *, n_blocks, tk):
    q1 = q1_ref[...]
    q2 = q2_ref[...]

    def body(i, carry):
        k = k_ref[pl.ds(pl.multiple_of(i * tk, tk), tk), :]
        return (jnp.maximum(carry[0], jnp.max(_dot_nt(k, q1), axis=0, keepdims=True)),
                jnp.maximum(carry[1], jnp.max(_dot_nt(k, q2), axis=0, keepdims=True)))

    start = jnp.full((1, q1.shape[0]), NEG_BIG, F32)
    m1, m2 = lax.fori_loop(0, n_blocks, body, (start, start))
    m1_ref[...] = m1
    m2_ref[...] = m2


def score_rowmax(q1, q2, k, tq, tk):
    b, h, nq, _ = q1.shape
    nk = k.shape[2]
    qs = pl.BlockSpec((None, None, tq, V7X_LANES), lambda i, j, s: (i, j, s, 0))
    ms = pl.BlockSpec((None, None, 1, tq), lambda i, j, s: (i, j, 0, s))
    sd = jax.ShapeDtypeStruct((b, h, 1, nq), F32)
    return pl.pallas_call(
        functools.partial(_rowmax_body, n_blocks=nk // tk, tk=tk),
        grid=(b, h, nq // tq),
        in_specs=[qs, qs, pl.BlockSpec((None, None, nk, V7X_LANES), lambda i, j, s: (i, j, 0, 0))],
        out_specs=[ms, ms],
        out_shape=[sd, sd],
        compiler_params=_cparams(("parallel", "parallel", "parallel"), 32),
        name="score_rowmax",
    )(q1, q2, k)


def score_bound(qg, kg):
    return 1.01 * DA_HEAD_DIM ** 0.5 * LOG2_E * jnp.max(jnp.abs(qg)) * jnp.max(jnp.abs(kg))


def softmax_shifts(bound, q1, q2, k, tq, tk):
    shape = q1.shape[:2] + (1, q1.shape[2])
    flat = lambda: (jnp.full(shape, bound, F32),) * 2
    return lax.cond(2.0 * bound <= F32_EXP2_RANGE, flat, lambda: tuple(score_rowmax(q1, q2, k, tq, tk)))


def _ret_body(qf_ref, kf_ref, vf_ref, gf_ref, cosf_ref, sinf_ref,
              qb_ref, kb_ref, vb_ref, gb_ref, cosb_ref, sinb_ref,
              dec_ref, intra_ref, seg_ref, s0_ref,
              yf_ref, yb_ref, sout_ref, state_ref, *, n_chunks, lc):
    step = pl.program_id(2)
    @pl.when(step == 0)
    def _():
        state_ref[...] = s0_ref[...]

    lane = lax.broadcasted_iota(jnp.int32, (1, V7X_LANES), 1)
    lower32 = (lane % RET_QK_DIM) < (RET_QK_DIM // 2)
    head0 = lane < RET_QK_DIM
    blk = (lax.broadcasted_iota(jnp.int32, (V7X_LANES, V7X_LANES), 0) // RET_QK_DIM ==
           lax.broadcasted_iota(jnp.int32, (V7X_LANES, V7X_LANES), 1) // RET_QK_DIM)
    seg = seg_ref[...]

    def rot(x, cos, sin):
        partner = jnp.where(lower32, pltpu.roll(x, 96, 1), pltpu.roll(x, 32, 1))
        return x * cos + partner * sin

    fwd_refs = (qf_ref, kf_ref, vf_ref, gf_ref, cosf_ref, sinf_ref, yf_ref)
    bwd_refs = (qb_ref, kb_ref, vb_ref, gb_ref, cosb_ref, sinb_ref, yb_ref)
    work = ([(0, c, fwd_refs) for c in range(n_chunks)] + [(1, c, bwd_refs) for c in reversed(range(n_chunks))])
    heads = (head0, jnp.logical_not(head0))

    def scores(d, c, refs):
        q_ref, k_ref, v_ref, _, cos_ref, sin_ref, _ = refs
        rows = pl.ds(c * lc, lc)
        cos = cos_ref[rows, :]
        sin = sin_ref[rows, :]
        q = rot(q_ref[rows, :], cos, sin)
        k = rot(k_ref[rows, :], cos, sin) * (RET_QK_DIM ** -0.5)
        v = v_ref[rows, :]
        kb = k.astype(BF16)
        sc = [_dot_nt(jnp.where(sel, q, 0.0).astype(BF16), kb) for sel in heads]
        kv = _dot_tn((k * dec_ref[d, 1]).astype(BF16), v.astype(BF16))
        return q.astype(BF16), v, sc, jnp.where(blk, kv, 0.0)

    stage1 = [scores(*w) for w in work]

    states = []
    for d in range(2):
        s = state_ref[d]
        cdec = dec_ref[d, 2][0:1, :]
        for (wd, _, _), (_, _, _, kv) in zip(work, stage1):
            if wd == d:
                states.append(s)
                s = s * cdec + kv
        state_ref[d] = s

    outs = []
    for (d, _, _), (qb, v, sc, _), s in zip(work, stage1, states):
        o = _dot(qb, s.astype(BF16)) * dec_ref[d, 0]
        for hh, sel in enumerate(heads):
            o = o + _dot((sc[hh] * intra_ref[d, hh]).astype(BF16), jnp.where(sel, v, 0.0).astype(BF16))
        outs.append(o)
    cens = [o - _seg_dot(o, seg) * (1.0 / RET_QK_DIM) for o in outs]
    variances = [_seg_dot(cen * cen, seg) * (1.0 / RET_QK_DIM) for cen in cens]
    for (_, c, refs), cen, var in zip(work, cens, variances):
        g = refs[3][pl.ds(c * lc, lc), :]
        refs[6][pl.ds(c * lc, lc), :] = cen * lax.rsqrt(var + NORM_EPS) * (g * _sigmoid(g))

    @pl.when(step == pl.num_programs(2) - 1)
    def _():
        sout_ref[...] = state_ref[...]


def retention(z, cos, sin, dec, intra, seg, s0, tm):
    b, r, _ = z.shape
    nb = r // tm
    lc = dec.shape[3]
    fwd = lambda off: pl.BlockSpec((None, tm, V7X_LANES), lambda i, p, k: (i, k, off + p))
    bwd = lambda off: pl.BlockSpec((None, tm, V7X_LANES), lambda i, p, k: (i, nb - 1 - k, off + p))
    tabf = pl.BlockSpec((tm, V7X_LANES), lambda i, p, k: (k, 0))
    tabb = pl.BlockSpec((tm, V7X_LANES), lambda i, p, k: (nb - 1 - k, 0))
    st = pl.BlockSpec((None, None, 2, V7X_LANES, V7X_LANES), lambda i, p, k: (i, p, 0, 0, 0))
    ysd = jax.ShapeDtypeStruct((b, r, 2 * V7X_LANES), F32)
    return pl.pallas_call(
        functools.partial(_ret_body, n_chunks=tm // lc, lc=lc),
        grid=(b, 2, nb),
        in_specs=[fwd(COL_RQ), fwd(COL_RK), fwd(COL_RV), fwd(COL_RGF), tabf, tabf,
                  bwd(COL_RQ), bwd(COL_RK), bwd(COL_RV), bwd(COL_RGB), tabb, tabb,
                  pl.BlockSpec((None, 2, 3, lc, V7X_LANES), lambda i, p, k: (p, 0, 0, 0, 0)),
                  pl.BlockSpec((None, 2, 2, lc, lc), lambda i, p, k: (p, 0, 0, 0, 0)),
                  pl.BlockSpec((V7X_LANES, V7X_LANES), lambda i, p, k: (0, 0)),
                  st],
        out_specs=[pl.BlockSpec((None, tm, V7X_LANES), lambda i, p, k: (i, k, p)),
                   pl.BlockSpec((None, tm, V7X_LANES), lambda i, p, k: (i, nb - 1 - k, p)),
                   st],
        out_shape=[ysd, ysd, jax.ShapeDtypeStruct(s0.shape, F32)],
        scratch_shapes=[pltpu.VMEM((2, V7X_LANES, V7X_LANES), F32)],
        compiler_params=_cparams(("parallel", "parallel", "arbitrary"), 32),
        name="retention",
    )(z, z, z, z, cos, sin, z, z, z, z, cos, sin, dec, intra, seg, s0)


def _sg_body(su_ref, sv_ref, ng_ref, w_ref, bm_ref, o_ref):
    tm = su_ref.shape[0]
    u = jax.nn.gelu(su_ref[...])
    t = jax.nn.gelu(sv_ref[...])
    mu = jnp.mean(t, axis=-1, keepdims=True)
    cen = t - mu
    var = jnp.mean(cen * cen, axis=-1, keepdims=True)
    v = (cen * lax.rsqrt(var + NORM_EPS) * ng_ref[...]).astype(BF16)
    group = lax.broadcasted_iota(jnp.int32, (1, 2 * V7X_LANES), 1) // SG_GROUP_DIM
    bias = bm_ref[...]
    for c in range(tm // SG_CHUNK):
        vc = v[c * SG_CHUNK:(c + 1) * SG_CHUNK, :]
        s = bias
        for g in range(SG_GROUPS):
            s = s + jnp.where(group == g, _dot(w_ref[g], vc), 0.0)
        o_ref[c * SG_CHUNK:(c + 1) * SG_CHUNK, :] = (u[c * SG_CHUNK:(c + 1) * SG_CHUNK, :] * s).astype(BF16)


def spatial_gating(z, ng, w, bm, tm):
    b, r, _ = z.shape
    wide = 2 * V7X_LANES
    return pl.pallas_call(
        _sg_body,
        grid=(b, r // tm),
        in_specs=[pl.BlockSpec((None, tm, wide), lambda i, j: (i, j, COL_SU256)),
                  pl.BlockSpec((None, tm, wide), lambda i, j: (i, j, COL_SV256)),
                  pl.BlockSpec((1, wide), lambda i, j: (0, 0)),
                  pl.BlockSpec((SG_GROUPS, SG_CHUNK, SG_CHUNK), lambda i, j: (0, 0, 0)),
                  pl.BlockSpec((SG_CHUNK, wide), lambda i, j: (0, 0))],
        out_specs=pl.BlockSpec((None, tm, wide), lambda i, j: (i, j, 0)),
        out_shape=jax.ShapeDtypeStruct((b, r, wide), BF16),
        compiler_params=_cparams(("parallel", "parallel"), 32),
        name="spatial_gating",
    )(z, z, ng, w, bm)


def _out_proj_body(a_ref, yf_ref, yb_ref, sg_ref, x_ref, g1_ref, a2_ref, s2_ref, w_ref, rwt_ref,
                   xo_ref, f_ref, aff_ref):
    na = a_ref.shape[1]
    nr = yf_ref.shape[1]
    r = (yf_ref[...] + yb_ref[...]).astype(BF16)
    y = (_dot(a_ref[...], w_ref[0:na, :]) + _dot(r, w_ref[na:na + nr, :])
         + _dot(sg_ref[...], w_ref[na + nr:, :]))
    xn = x_ref[...] + g1_ref[...] * y
    xo_ref[...] = xn
    ms = jnp.mean(xn * xn, axis=-1, keepdims=True)
    f = (xn * lax.rsqrt(ms + NORM_EPS) * a2_ref[...] + s2_ref[...]).astype(BF16)
    f_ref[...] = f
    logits = _dot_nt(rwt_ref[...], f)
    e = jnp.exp(logits - jnp.max(logits, axis=0, keepdims=True))
    aff_ref[...] = e / jnp.sum(e, axis=0, keepdims=True)


def out_proj(a, yf, yb, sg, x, g1, a2, s2, w, rwt, tm):
    b, r, d = x.shape
    rows = lambda width: pl.BlockSpec((None, tm, width), lambda i, j: (i, j, 0))
    vec = pl.BlockSpec((None, 1, d), lambda i, j: (i, 0, 0))
    return pl.pallas_call(
        _out_proj_body,
        grid=(b, r // tm),
        in_specs=[rows(a.shape[2]), rows(yf.shape[2]), rows(yb.shape[2]), rows(sg.shape[2]), rows(d),
                  vec, vec, vec,
                  pl.BlockSpec(w.shape, lambda i, j: (0, 0)),
                  pl.BlockSpec(rwt.shape, lambda i, j: (0, 0))],
        out_specs=[rows(d), rows(d), pl.BlockSpec((None, N_EXPERTS, tm), lambda i, j: (i, 0, j))],
        out_shape=[jax.ShapeDtypeStruct((b, r, d), F32), jax.ShapeDtypeStruct((b, r, d), BF16),
                   jax.ShapeDtypeStruct((b, N_EXPERTS, r), F32)],
        compiler_params=_cparams(("parallel", "parallel"), 32),
        name="out_proj",
    )(a, yf, yb, sg, x, g1, a2, s2, w, rwt)


def _router_body(aff_ref, tri_ref, posm_ref, off_ref, *, cap, tb):
    aff = aff_ref[...]
    e, r = aff.shape
    nb = r // tb
    bits = pltpu.bitcast(aff, jnp.int32)

    def search(i, thr):
        cand = thr | lax.shift_left(jnp.int32(1), 30 - i)
        cnt = jnp.sum(jnp.where(bits >= cand, 1.0, 0.0), axis=1, keepdims=True)
        return jnp.where(cnt >= cap, cand, thr)

    thr = lax.fori_loop(0, 31, search, jnp.zeros((e, 1), jnp.int32))
    gt = jnp.where(bits > thr, 1.0, 0.0)
    eq = jnp.where(bits == thr, 1.0, 0.0)
    need = cap - jnp.sum(gt, axis=1, keepdims=True)
    tri = tri_ref[...]
    lane = lax.broadcasted_iota(jnp.int32, (e, V7X_LANES), 1)
    offs = jnp.zeros((e, V7X_LANES), jnp.int32)
    run_eq = jnp.zeros((e, 1), F32)
    run_sel = jnp.zeros((e, 1), F32)
    for j in range(nb):
        cols = slice(j * tb, (j + 1) * tb)
        eq_j = eq[:, cols]
        before_eq = _dot(eq_j.astype(BF16), tri) + run_eq
        sel_j = gt[:, cols] + eq_j * jnp.where(before_eq < need, 1.0, 0.0)
        pos_j = _dot(sel_j.astype(BF16), tri) + run_sel
        posm_ref[:, cols] = jnp.where(sel_j > 0.0, pos_j, -1.0)
        offs = jnp.where(lane == j, run_sel.astype(jnp.int32), offs)
        run_eq = run_eq + jnp.sum(eq_j, axis=1, keepdims=True)
        run_sel = run_sel + jnp.sum(sel_j, axis=1, keepdims=True)
    off_ref[...] = jnp.where(lane == nb, run_sel.astype(jnp.int32), offs)


def route(aff_t, tri, cap):
    b, e, r = aff_t.shape
    tb = tri.shape[0]
    posm, off = pl.pallas_call(
        functools.partial(_router_body, cap=cap, tb=tb),
        grid=(b,),
        in_specs=[pl.BlockSpec((None, e, r), lambda i: (i, 0, 0)),
                  pl.BlockSpec((tb, tb), lambda i: (0, 0))],
        out_specs=[pl.BlockSpec((None, e, r), lambda i: (i, 0, 0)),
                   pl.BlockSpec((None, e, V7X_LANES), lambda i: (i, 0, 0))],
        out_shape=[jax.ShapeDtypeStruct((b, e, r), F32), jax.ShapeDtypeStruct((b, e, V7X_LANES), jnp.int32)],
        compiler_params=_cparams(("parallel",), 32),
        name="router",
    )(aff_t, tri)
    shape5 = (b, e, r // tb, 1, tb)
    return posm.reshape(shape5), aff_t.reshape(shape5), off.reshape(-1)


def _gather_body(off_ref, f_hbm, posm_ref, aff_ref, x_ref, g_ref, fbuf, xbuf, gbuf, sem, *, cap, w, tb, nb, unroll):
    b = pl.program_id(0)
    e = pl.program_id(1)

    @pl.when(e == 0)
    def _():
        cp = pltpu.make_async_copy(f_hbm.at[b], fbuf, sem)
        cp.start()
        cp.wait()

    base = (b * pl.num_programs(1) + e) * V7X_LANES
    iota = lax.broadcasted_iota(jnp.int32, (w, 1), 0).astype(F32)
    align = 8
    xbuf[...] = jnp.zeros(xbuf.shape, F32)
    gbuf[...] = jnp.zeros(gbuf.shape, F32)

    def window(j, rnd):
        return pl.multiple_of((off_ref[base + j] // align) * align + rnd * w, align)

    def rows_of(j, rnd):
        onehot = posm_ref[j] == window(j, rnd).astype(F32) + iota
        toks = fbuf[pl.ds(pl.multiple_of(j * tb, tb), tb), :]
        return (_dot(jnp.where(onehot, 1.0, 0.0).astype(BF16), toks),
                jnp.sum(jnp.where(onehot, aff_ref[j], 0.0), axis=1, keepdims=True))

    def add(j, rnd, vals):
        rows = pl.ds(window(j, rnd), w)
        xbuf[rows, :] += vals[0]
        gbuf[rows, :] += vals[1]

    def span(j):
        return off_ref[base + j + 1] - (off_ref[base + j] // align) * align

    def first_round(i, widest):
        js = [i * unroll + u for u in range(unroll)]
        vals = [rows_of(j, 0) for j in js]
        for j, v in zip(js, vals):
            add(j, 0, v)
            widest = jnp.maximum(widest, span(j))
        return widest

    widest = lax.fori_loop(0, nb // unroll, first_round, jnp.int32(0))

    @pl.when(widest > w)
    def _():
        def more_rounds(j, carry):
            def one(rnd, c):
                add(j, rnd, rows_of(j, rnd))
                return c

            return lax.fori_loop(1, (span(j) + w - 1) // w, one, carry)

        lax.fori_loop(0, nb, more_rounds, 0)
    x_ref[...] = xbuf[0:cap, :].astype(BF16)
    g_ref[...] = gbuf[0:cap, :]


def moe_gather(off, f, posm, aff, cap, w):
    b, r, d = f.shape
    _, e, nb, _, tb = posm.shape
    unroll = math.gcd(4, nb)
    blk = pl.BlockSpec((None, None, nb, 1, tb), lambda i, j, o: (i, j, 0, 0, 0))
    return pl.pallas_call(
        functools.partial(_gather_body, cap=cap, w=w, tb=tb, nb=nb, unroll=unroll),
        grid_spec=pltpu.PrefetchScalarGridSpec(
            num_scalar_prefetch=1,
            grid=(b, e),
            in_specs=[pl.BlockSpec(memory_space=pl.ANY), blk, blk],
            out_specs=[pl.BlockSpec((None, None, cap, d), lambda i, j, o: (i, j, 0, 0)),
                       pl.BlockSpec((None, None, cap, 1), lambda i, j, o: (i, j, 0, 0))],
            scratch_shapes=[pltpu.VMEM((r, d), BF16), pltpu.VMEM((cap + w, d), F32), pltpu.VMEM((cap + w, 1), F32),
                            pltpu.SemaphoreType.DMA(())]),
        out_shape=[jax.ShapeDtypeStruct((b, e, cap, d), BF16), jax.ShapeDtypeStruct((b, e, cap, 1), F32)],
        compiler_params=_cparams(("arbitrary", "arbitrary"), 40),
        name="moe_gather",
    )(off, f, posm, aff)


def _moe_ffn_body(*refs, has_ctx, cap, capc):
    if has_ctx:
        _, x_ref, g_ref, xc_ref, gc_ref, wg_ref, wu_ref, wd_ref, y_ref, yc_ref, yacc, ycacc = refs
    else:
        _, x_ref, g_ref, wg_ref, wu_ref, wd_ref, y_ref, yacc = refs
    c = pl.program_id(1)
    last = pl.num_programs(1) - 1
    nbatch, _, d = x_ref.shape
    wg = wg_ref[...].astype(BF16)
    wu = wu_ref[...].astype(BF16)
    wd = wd_ref[...].astype(BF16)

    def hidden(x):
        a = _dot(x, wg)
        u = _dot(x, wu)
        return (a * _sigmoid(a) * u).astype(BF16)

    @pl.when(c == 0)
    def _():
        yacc[...] = jnp.zeros(yacc.shape, F32)
        if has_ctx:
            ycacc[...] = jnp.zeros(ycacc.shape, F32)

    hs = [hidden(x_ref[i]) for i in range(nbatch)]
    if has_ctx:
        hc = hidden(xc_ref[...].reshape(nbatch * capc, d))
    for i in range(nbatch):
        yacc[i] += _dot(hs[i], wd)
    if has_ctx:
        ycacc[...] += _dot(hc, wd)

    @pl.when(c == last)
    def _():
        for i in range(nbatch):
            y_ref[i, 0:cap, :] = (yacc[i] * g_ref[i]).astype(BF16)
            y_ref[i, cap:, :] = jnp.zeros((y_ref.shape[1] - cap, d), BF16)
        if has_ctx:
            yc = ycacc[...] * gc_ref[...].reshape(nbatch * capc, 1)
            yc_ref[:, 0:capc, :] = yc.reshape(nbatch, capc, d).astype(BF16)
            yc_ref[:, capc:, :] = jnp.zeros((nbatch, yc_ref.shape[1] - capc, d), BF16)


def moe_ffn(layer, x, g, xc, gc, wg, wu, wd, tf, pad, padc):
    b, e, cap, d = x.shape
    ff = wg.shape[3]
    has_ctx = xc is not None
    capc = xc.shape[2] if has_ctx else 0
    tok = lambda rows, width: pl.BlockSpec((b, None, rows, width), lambda j, k, l: (0, j, 0, 0))
    in_specs = [tok(cap, d), tok(cap, 1)]
    out_specs = [tok(cap + pad, d)]
    out_shape = [jax.ShapeDtypeStruct((b, e, cap + pad, d), BF16)]
    scratch = [pltpu.VMEM((b, cap, d), F32)]
    args = [x, g]
    if has_ctx:
        in_specs += [tok(capc, d), tok(capc, 1)]
        out_specs.append(tok(capc + padc, d))
        out_shape.append(jax.ShapeDtypeStruct((b, e, capc + padc, d), BF16))
        scratch.append(pltpu.VMEM((b * capc, d), F32))
        args += [xc, gc]
    in_specs += [pl.BlockSpec((None, None, d, tf), lambda j, k, l: (l[0], j, 0, k)),
                 pl.BlockSpec((None, None, d, tf), lambda j, k, l: (l[0], j, 0, k)),
                 pl.BlockSpec((None, None, tf, d), lambda j, k, l: (l[0], j, k, 0))]
    out = pl.pallas_call(
        functools.partial(_moe_ffn_body, has_ctx=has_ctx, cap=cap, capc=capc),
        grid_spec=pltpu.PrefetchScalarGridSpec(
            num_scalar_prefetch=1, grid=(e, ff // tf),
            in_specs=in_specs, out_specs=out_specs, scratch_shapes=scratch),
        out_shape=out_shape,
        compiler_params=_cparams(("parallel", "arbitrary"), 56),
        name="moe_ffn",
    )(layer, *args, wg, wu, wd)
    return out if has_ctx else (out[0], None)


def _scatter_body(off_ref, x_ref, g2_ref, posm_ref, y_hbm, xo_ref, ybuf, pcat, sem, *, cap, w, nb, n_exp):
    b = pl.program_id(0)
    j = pl.program_id(1)
    step = b * nb + j
    slot = lax.rem(step, 2)
    spare = 2
    align = V7X_BF16_SUBLANES

    def window(bb, jj, e, rnd):
        off = off_ref[(bb * n_exp + e) * V7X_LANES + jj]
        return jnp.minimum((off // align) * align + rnd * w, cap)

    def copies(bb, jj, buf, rnd):
        return [pltpu.make_async_copy(
            y_hbm.at[bb, e, pl.ds(pl.multiple_of(window(bb, jj, e, rnd), align), w), :],
            ybuf.at[buf, pl.ds(e * w, w), :], sem.at[buf]) for e in range(n_exp)]

    @pl.when(step == 0)
    def _():
        for cp in copies(b, j, 0, 0):
            cp.start()

    @pl.when(step + 1 < pl.num_programs(0) * nb)
    def _():
        nxt = step + 1
        for cp in copies(nxt // nb, lax.rem(nxt, nb), 1 - slot, 0):
            cp.start()

    iota = lax.broadcasted_iota(jnp.int32, (w, 1), 0).astype(F32)

    def onehots(rnd):
        for e in range(n_exp):
            first = window(b, j, e, rnd).astype(F32)
            pcat[e * w:(e + 1) * w, :] = jnp.where(posm_ref[e] == first + iota, 1.0, 0.0).astype(BF16)

    for cp in copies(b, j, slot, 0):
        cp.wait()
    onehots(0)
    g2 = g2_ref[...]
    xo_ref[...] = x_ref[...] + g2 * _dot_tn(pcat[...], ybuf[slot])

    span = jnp.int32(0)
    for e in range(n_exp):
        idx = (b * n_exp + e) * V7X_LANES + j
        span = jnp.maximum(span, off_ref[idx + 1] - (off_ref[idx] // align) * align)

    def extra(rnd, carry):
        for cp in copies(b, j, spare, rnd):
            cp.start()
        for cp in copies(b, j, spare, rnd):
            cp.wait()
        onehots(rnd)
        xo_ref[...] += g2 * _dot_tn(pcat[...], ybuf[spare])
        return carry

    lax.fori_loop(1, (span + w - 1) // w, extra, 0)


def moe_scatter(off, x, g2, posm, y, cap, w):
    b, r, d = x.shape
    _, e, nb, _, tb = posm.shape
    rows = pl.BlockSpec((None, tb, d), lambda i, j, o: (i, j, 0))
    return pl.pallas_call(
        functools.partial(_scatter_body, cap=cap, w=w, nb=nb, n_exp=e),
        grid_spec=pltpu.PrefetchScalarGridSpec(
            num_scalar_prefetch=1,
            grid=(b, nb),
            in_specs=[rows, pl.BlockSpec((None, 1, d), lambda i, j, o: (i, 0, 0)),
                      pl.BlockSpec((None, e, None, 1, tb), lambda i, j, o: (i, 0, j, 0, 0)),
                      pl.BlockSpec(memory_space=pl.ANY)],
            out_specs=rows,
            scratch_shapes=[pltpu.VMEM((3, e * w, d), BF16), pltpu.VMEM((e * w, tb), BF16),
                            pltpu.SemaphoreType.DMA((3,))]),
        out_shape=jax.ShapeDtypeStruct((b, r, d), F32),
        compiler_params=_cparams(("arbitrary", "arbitrary"), 32),
        name="moe_scatter",
    )(off, x, g2, posm, y)


def _attn_tables(n):
    rows = n // GRID_W
    row = jnp.broadcast_to(jnp.arange(rows, dtype=F32)[:, None], (rows, GRID_W)).reshape(-1)
    col = jnp.broadcast_to(jnp.arange(GRID_W, dtype=F32)[None, :], (rows, GRID_W)).reshape(-1)
    quarter = DA_HEAD_DIM // 4
    inv = ROPE_BASE ** (-jnp.arange(quarter, dtype=F32) / quarter)
    ang_r, ang_c = row[:, None] * inv, col[:, None] * inv

    def half(ang):
        return (jnp.concatenate([jnp.cos(ang), jnp.cos(ang)], -1),
                jnp.concatenate([-jnp.sin(ang), jnp.sin(ang)], -1))

    cr, sr = half(ang_r)
    cc, sc = half(ang_c)
    cos64 = jnp.concatenate([cr, cc], -1)
    sin64 = jnp.concatenate([sr, sc], -1)
    return jnp.tile(cos64, (1, 2)), jnp.tile(sin64, (1, 2))


def _ret_tables(n):
    inv = 1.0 / (ROPE_BASE ** jnp.linspace(0.0, 1.0, RET_QK_DIM // 2, dtype=F32))
    ang = jnp.arange(n, dtype=F32)[:, None] * inv
    cos64 = jnp.concatenate([jnp.cos(ang), jnp.cos(ang)], -1)
    sin64 = jnp.concatenate([-jnp.sin(ang), jnp.sin(ang)], -1)
    return jnp.tile(cos64, (1, 2)), jnp.tile(sin64, (1, 2))


def _ret_decay_tables(log_gamma, lc):
    pos = jnp.arange(lc, dtype=F32)
    lg = jnp.repeat(log_gamma, RET_QK_DIM, axis=1).reshape(2, 2, 1, V7X_LANES)
    qd = jnp.stack([jnp.exp(lg[0] * (pos + 1.0)[None, :, None]), jnp.exp(lg[1] * (lc - pos)[None, :, None])])
    kd = jnp.stack([jnp.exp(lg[0] * (lc - 1.0 - pos)[None, :, None]), jnp.exp(lg[1] * pos[None, :, None])])
    cd = jnp.broadcast_to(jnp.exp(lg * lc), (2, 2, lc, V7X_LANES))
    dec = jnp.stack([qd, kd, cd], axis=2)
    dist = pos[:, None] - pos[None, :]
    lgh = log_gamma[:, :, None, None]
    fwd = jnp.where(dist >= 0, jnp.exp(lgh[0] * jnp.maximum(dist, 0.0)), 0.0)
    bwd = jnp.where(dist <= 0, jnp.exp(lgh[1] * jnp.maximum(-dist, 0.0)), 0.0)
    intra = jnp.stack([fwd, bwd]).reshape(2, 2, 2, lc, lc)
    return jnp.swapaxes(dec, 0, 1), jnp.swapaxes(intra, 0, 1)


def kernel(x, c, ctx, c_ctx, w_mod, b_mod, norm1_g, norm2_g, w_in, w_out, da_q_norm_g, da_k_norm_g,
           da_lambda_q1, da_lambda_k1, da_lambda_q2, da_lambda_k2, da_subln_g, ret_decay,
           sg_norm_g, sg_w, sg_b, router_w, ex_w_gate, ex_w_up, ex_w_down):
    b, n, d = x.shape
    n_ctx = ctx.shape[1]
    cos_a, sin_a = _attn_tables(n)
    cos_r, sin_r = _ret_tables(n)
    one_c = jnp.ones((n_ctx, V7X_LANES), F32)
    zero_c = jnp.zeros((n_ctx, V7X_LANES), F32)
    lane = jnp.arange(V7X_LANES)
    seg = (lane[:, None] // 64 == lane[None, :] // 64).astype(BF16)
    state0 = jnp.zeros((b, 2, 2, V7X_LANES, V7X_LANES), F32)
    tk = n_ctx
    assert (n + n_ctx) // tk % ATTN_BLOCKS_PER_STEP == 0
    tok = jnp.arange(MOE_TOKEN_BLOCK)
    tri = (tok[:, None] < tok[None, :]).astype(BF16)
    cap = EC_CAPACITY * n // N_EXPERTS
    cap_c = EC_CAPACITY * n_ctx // N_EXPERTS
    pad, pad_c = min(64, cap), min(64, cap_c)

    for layer in range(DEPTH):
        need_ctx = layer < DEPTH - 1
        lambda_init = 0.8 - 0.6 * math.exp(-0.3 * layer)
        mod = jax.nn.silu(c) @ w_mod[layer] + b_mod[layer]
        mod_c = jnp.broadcast_to(jax.nn.silu(c_ctx) @ w_mod[layer] + b_mod[layer], mod.shape)
        sh1, sc1, g1, sh2, sc2, g2 = jnp.split(mod[:, None, :], 6, axis=-1)
        csh1, csc1, cg1, csh2, csc2, cg2 = jnp.split(mod_c[:, None, :], 6, axis=-1)
        lam = (jnp.exp(jnp.sum(da_lambda_q1[layer] * da_lambda_k1[layer]))
               - jnp.exp(jnp.sum(da_lambda_q2[layer] * da_lambda_k2[layer])) + lambda_init).reshape(1)
        log_gamma = jax.nn.log_sigmoid(ret_decay[layer].astype(F32))
        dec, intra = _ret_decay_tables(log_gamma, RET_SCAN_CHUNK)
        w_in_b = w_in[layer].astype(BF16)
        w_out_b = w_out[layer].astype(BF16)
        rwt = router_w[layer].T.astype(BF16)
        qg = jnp.tile(da_q_norm_g[layer], 2)[None, :]
        kg = jnp.tile(da_k_norm_g[layer], 2)[None, :]
        gcol = (da_subln_g[layer] * (1.0 - lambda_init))[:, None]
        sg_ng = sg_norm_g[layer][None, :]
        sg_wb = sg_w[layer].astype(BF16)
        sg_bm = jnp.repeat(sg_b[layer].T, SG_GROUP_DIM, axis=1)
        n1, n2 = norm1_g[layer][None, None, :], norm2_g[layer][None, None, :]

        z_lat = in_proj(x, n1 * (1.0 + sc1), sh1, w_in_b, 512)
        z_ctx = in_proj(ctx, n1 * (1.0 + csc1), csh1, w_in_b, n_ctx)

        cq1, cq2, ck, cvt = qkv_prep(z_ctx, one_c, zero_c, qg, kg, seg, n_ctx, tk)
        q1, q2, kl, vlt = qkv_prep(z_lat, cos_a, sin_a, qg, kg, seg, _row_tile(n, 1024), tk)
        k_all = jnp.concatenate([ck, kl], axis=2)
        vt_all = jnp.concatenate([cvt, vlt], axis=2)
        bound = score_bound(da_q_norm_g[layer], da_k_norm_g[layer])
        c1, c2 = softmax_shifts(bound, q1, q2, k_all, ATTN_QUERY_TILE, tk)
        a_lat = diff_attention(lam, q1, q2, c1, c2, k_all, vt_all, gcol, ATTN_QUERY_TILE, ATTN_BLOCKS_PER_STEP)

        cyf, cyb, cstate = retention(z_ctx, one_c, zero_c, dec, intra, seg, state0, n_ctx)
        yf, yb, _ = retention(z_lat, cos_r, sin_r, dec, intra, seg, cstate, 512)

        s_lat = spatial_gating(z_lat, sg_ng, sg_wb, sg_bm, 512)

        x, f_lat, aff_lat = out_proj(a_lat, yf, yb, s_lat, x, g1, n2 * (1.0 + sc2), sh2, w_out_b, rwt, 512)
        posm, aff5, off = route(aff_lat, tri, cap)
        xs, gs = moe_gather(off, f_lat, posm, aff5, cap, pad)
        xcs = gcs = None
        if need_ctx:
            cc1, cc2 = softmax_shifts(bound, cq1, cq2, ck, n_ctx, tk)
            a_ctx = diff_attention(lam, cq1, cq2, cc1, cc2, ck, cvt, gcol, n_ctx, 1)
            s_ctx = spatial_gating(z_ctx, sg_ng, sg_wb, sg_bm, n_ctx)
            ctx, f_ctx, aff_ctx = out_proj(a_ctx, cyf, cyb, s_ctx, ctx, cg1, n2 * (1.0 + csc2), csh2,
                                           w_out_b, rwt, n_ctx)
            posm_c, aff5_c, off_c = route(aff_ctx, tri, cap_c)
            xcs, gcs = moe_gather(off_c, f_ctx, posm_c, aff5_c, cap_c, pad_c)
        ys, ycs = moe_ffn(jnp.full((1,), layer, jnp.int32), xs, gs, xcs, gcs, ex_w_gate, ex_w_up, ex_w_down,
                          512, pad, pad_c)
        x = moe_scatter(off, x, g2, posm, ys, cap, pad)
        if need_ctx:
            ctx = moe_scatter(off_c, ctx, cg2, posm_c, ycs, cap_c, pad_c)
    return x
```

```python
import functools
import math

import jax
import jax.numpy as jnp
from jax import lax
from jax.experimental import pallas as pl
from jax.experimental.pallas import tpu as pltpu

F32 = jnp.float32
BF16 = jnp.bfloat16

D_MODEL = 1024
DEPTH = 4
GRID_W = 64
NORM_EPS = 1e-6
ROPE_BASE = 10000.0
DA_HEADS = 4
DA_HEAD_DIM = 64
RET_HEADS = 4
RET_QK_DIM = 64
RET_SCAN_CHUNK = 128
SG_GROUPS = 4
SG_GROUP_DIM = 64
SG_CHUNK = 128
N_EXPERTS = 16
EXPERT_FF = 2 * D_MODEL
EC_CAPACITY = 2
IN_COLS = 3328
COL_AQ, COL_AK, COL_AV = 0, 4, 8
COL_RQ, COL_RK, COL_RV, COL_RGF, COL_RGB = 12, 14, 16, 18, 20
COL_SU256, COL_SV256 = 11, 12

V7X_LANES = 128
V7X_BF16_SUBLANES = 16
V7X_VMEM_BYTES = 64 * 1024 * 1024
MOE_TOKEN_BLOCK = 256
ATTN_BLOCKS_PER_STEP = 11
ATTN_QUERY_TILE = 512
NEG_BIG = -1e30
LOG2_E = 1.4426950408889634
F32_EXP2_RANGE = 120.0


def _cparams(semantics, vmem_mib):
    assert vmem_mib * 1024 * 1024 < V7X_VMEM_BYTES
    return pltpu.CompilerParams(dimension_semantics=semantics, vmem_limit_bytes=vmem_mib * 1024 * 1024)


def _row_tile(rows, preferred):
    while rows % preferred:
        preferred //= 2
    return preferred


def _dot(a, b):
    return jnp.dot(a, b, preferred_element_type=F32)


def _dot_nt(a, b):
    return lax.dot_general(a, b, (((1,), (1,)), ((), ())), preferred_element_type=F32)


def _dot_tn(a, b):
    return lax.dot_general(a, b, (((0,), (0,)), ((), ())), preferred_element_type=F32)


def _seg_dot(x, seg):
    hi = x.astype(BF16)
    lo = (x - hi.astype(F32)).astype(BF16)
    return _dot(hi, seg) + _dot(lo, seg)


def _sigmoid(x):
    return 1.0 / (1.0 + jnp.exp(-x))


def _in_proj_body(x_ref, a_ref, s_ref, w_ref, z_ref):
    x = x_ref[...]
    ms = jnp.mean(x * x, axis=-1, keepdims=True)
    h = x * lax.rsqrt(ms + NORM_EPS) * a_ref[...] + s_ref[...]
    z_ref[...] = _dot(h.astype(BF16), w_ref[...])


def in_proj(x, a, s, w, tm):
    b, r, d = x.shape
    c = w.shape[1]
    return pl.pallas_call(
        _in_proj_body,
        grid=(b, r // tm),
        in_specs=[
            pl.BlockSpec((None, tm, d), lambda i, j: (i, j, 0)),
            pl.BlockSpec((None, 1, d), lambda i, j: (i, 0, 0)),
            pl.BlockSpec((None, 1, d), lambda i, j: (i, 0, 0)),
            pl.BlockSpec((d, c), lambda i, j: (0, 0)),
        ],
        out_specs=pl.BlockSpec((None, tm, c), lambda i, j: (i, j, 0)),
        out_shape=jax.ShapeDtypeStruct((b, r, c), F32),
        compiler_params=_cparams(("parallel", "parallel"), 48),
        name="in_proj",
    )(x, a, s, w)


def _qkv_prep_body(q_ref, k_ref, v_ref, cos_ref, sin_ref, qg_ref, kg_ref, seg_ref,
                   qp1_ref, qp2_ref, kk_ref, vt_ref):
    lane = lax.broadcasted_iota(jnp.int32, (1, V7X_LANES), 1)
    lower16 = (lane % 32) < 16
    seg = seg_ref[...]
    cos = cos_ref[...]
    sin = sin_ref[...]

    def norm_rope(x, g):
        ms = _seg_dot(x * x, seg) * (1.0 / DA_HEAD_DIM)
        y = x * lax.rsqrt(ms + NORM_EPS) * g
        partner = jnp.where(lower16, pltpu.roll(y, 112, 1), pltpu.roll(y, 16, 1))
        return y * cos + partner * sin

    q = norm_rope(q_ref[...], qg_ref[...]) * (DA_HEAD_DIM ** -0.5 * LOG2_E)
    first = lane < DA_HEAD_DIM
    qp1_ref[...] = jnp.where(first, q, 0.0).astype(BF16)
    qp2_ref[...] = jnp.where(first, 0.0, q).astype(BF16)
    kk_ref[...] = norm_rope(k_ref[...], kg_ref[...]).astype(BF16)
    tv = vt_ref.shape[2]
    for s in range(vt_ref.shape[0]):
        vt_ref[s] = v_ref[s * tv:(s + 1) * tv, :].T.astype(BF16)


def qkv_prep(z, cos, sin, qg, kg, seg, tm, tv):
    b, r, _ = z.shape
    h = DA_HEADS
    row = lambda off: pl.BlockSpec((None, tm, V7X_LANES), lambda i, j, k: (i, k, off + j))
    tab = pl.BlockSpec((tm, V7X_LANES), lambda i, j, k: (k, 0))
    vec = pl.BlockSpec((1, V7X_LANES), lambda i, j, k: (0, 0))
    o4 = pl.BlockSpec((None, None, tm, V7X_LANES), lambda i, j, k: (i, j, k, 0))
    sd = jax.ShapeDtypeStruct((b, h, r, V7X_LANES), BF16)
    return pl.pallas_call(
        _qkv_prep_body,
        grid=(b, h, r // tm),
        in_specs=[row(COL_AQ), row(COL_AK), row(COL_AV), tab, tab, vec, vec,
                  pl.BlockSpec((V7X_LANES, V7X_LANES), lambda i, j, k: (0, 0))],
        out_specs=[o4, o4, o4,
                   pl.BlockSpec((None, None, tm // tv, V7X_LANES, tv), lambda i, j, k: (i, j, k, 0, 0))],
        out_shape=[sd, sd, sd, jax.ShapeDtypeStruct((b, h, r // tv, V7X_LANES, tv), BF16)],
        compiler_params=_cparams(("parallel", "parallel", "parallel"), 32),
        name="qkv_prep",
    )(z, z, z, cos, sin, qg, kg, seg)


def _attn_body(lam_ref, q1_ref, q2_ref, c1_ref, c2_ref, k_ref, v_ref, g_ref, o_ref, acc_ref, *, n_steps, bps):
    q1 = q1_ref[...]
    q2 = q2_ref[...]
    c1 = c1_ref[...]
    c2 = c2_ref[...]
    tq = q1.shape[0]
    tk = v_ref.shape[2]
    keys = bps * tk

    def probs(i):
        k = k_ref[i * keys:(i + 1) * keys, :]
        p1 = jnp.exp2(_dot_nt(k, q1) - c1)
        p2 = jnp.exp2(_dot_nt(k, q2) - c2)
        return (p1.astype(BF16), p2.astype(BF16),
                jnp.sum(p1.reshape(-1, 8, tq), axis=0), jnp.sum(p2.reshape(-1, 8, tq), axis=0))

    def accumulate(i, p1, p2):
        for idx, p in ((0, p1), (1, p2)):
            part = _dot(v_ref[i * bps], p[0:tk])
            for s in range(1, bps):
                part = part + _dot(v_ref[i * bps + s], p[s * tk:(s + 1) * tk])
            acc_ref[idx] += part

    acc_ref[...] = jnp.zeros(acc_ref.shape, F32)
    p1, p2, l1, l2 = probs(0)
    for i in range(n_steps - 1):
        n1, n2, d1, d2 = probs(i + 1)
        accumulate(i, p1, p2)
        p1, p2, l1, l2 = n1, n2, l1 + d1, l2 + d2
    accumulate(n_steps - 1, p1, p2)
    l1 = jnp.sum(l1, axis=0, keepdims=True)
    l2 = jnp.sum(l2, axis=0, keepdims=True)
    o = acc_ref[0] / l1 - lam_ref[0] * (acc_ref[1] / l2)
    ms = jnp.mean(o * o, axis=0, keepdims=True)
    y = o * lax.rsqrt(ms + NORM_EPS) * g_ref[...]
    o_ref[...] = y.T.astype(BF16)


def diff_attention(lam, q1, q2, c1, c2, k, vt, gcol, tq, bps):
    b, h, nq, _ = q1.shape
    nk = k.shape[2]
    nblk, _, tk = vt.shape[2:]
    qs = pl.BlockSpec((None, None, tq, V7X_LANES), lambda i, j, s: (i, j, s, 0))
    cs = pl.BlockSpec((None, None, 1, tq), lambda i, j, s: (i, j, 0, s))
    return pl.pallas_call(
        functools.partial(_attn_body, n_steps=nblk // bps, bps=bps),
        grid=(b, h, nq // tq),
        in_specs=[pl.BlockSpec(memory_space=pltpu.SMEM), qs, qs, cs, cs,
                  pl.BlockSpec((None, None, nk, V7X_LANES), lambda i, j, s: (i, j, 0, 0)),
                  pl.BlockSpec((None, None, nblk, V7X_LANES, tk), lambda i, j, s: (i, j, 0, 0, 0)),
                  pl.BlockSpec((V7X_LANES, 1), lambda i, j, s: (0, 0))],
        out_specs=pl.BlockSpec((None, tq, V7X_LANES), lambda i, j, s: (i, s, j)),
        out_shape=jax.ShapeDtypeStruct((b, nq, h * V7X_LANES), BF16),
        scratch_shapes=[pltpu.VMEM((2, V7X_LANES, tq), F32)],
        compiler_params=_cparams(("parallel", "parallel", "parallel"), 48),
        name="diff_attention",
    )(lam, q1, q2, c1, c2, k, vt, gcol)


def _rowmax_body(q1_ref, q2_ref, k_ref, m1_ref, m2_ref, *, n_blocks, tk):
    q1 = q1_ref[...]
    q2 = q2_ref[...]

    def body(i, carry):
        k = k_ref[pl.ds(pl.multiple_of(i * tk, tk), tk), :]
        return (jnp.maximum(carry[0], jnp.max(_dot_nt(k, q1), axis=0, keepdims=True)),
                jnp.maximum(carry[1], jnp.max(_dot_nt(k, q2), axis=0, keepdims=True)))

    start = jnp.full((1, q1.shape[0]), NEG_BIG, F32)
    m1, m2 = lax.fori_loop(0, n_blocks, body, (start, start))
    m1_ref[...] = m1
    m2_ref[...] = m2


def score_rowmax(q1, q2, k, tq, tk):
    b, h, nq, _ = q1.shape
    nk = k.shape[2]
    qs = pl.BlockSpec((None, None, tq, V7X_LANES), lambda i, j, s: (i, j, s, 0))
    ms = pl.BlockSpec((None, None, 1, tq), lambda i, j, s: (i, j, 0, s))
    sd = jax.ShapeDtypeStruct((b, h, 1, nq), F32)
    return pl.pallas_call(
        functools.partial(_rowmax_body, n_blocks=nk // tk, tk=tk),
        grid=(b, h, nq // tq),
        in_specs=[qs, qs, pl.BlockSpec((None, None, nk, V7X_LANES), lambda i, j, s: (i, j, 0, 0))],
        out_specs=[ms, ms],
        out_shape=[sd, sd],
        compiler_params=_cparams(("parallel", "parallel", "parallel"), 32),
        name="score_rowmax",
    )(q1, q2, k)


def score_bound(qg, kg):
    return 1.01 * DA_HEAD_DIM ** 0.5 * LOG2_E * jnp.max(jnp.abs(qg)) * jnp.max(jnp.abs(kg))


def softmax_shifts(bound, q1, q2, k, tq, tk):
    shape = q1.shape[:2] + (1, q1.shape[2])
    flat = lambda: (jnp.full(shape, bound, F32),) * 2
    return lax.cond(2.0 * bound <= F32_EXP2_RANGE, flat, lambda: tuple(score_rowmax(q1, q2, k, tq, tk)))


def _ret_body(qf_ref, kf_ref, vf_ref, gf_ref, cosf_ref, sinf_ref,
              qb_ref, kb_ref, vb_ref, gb_ref, cosb_ref, sinb_ref,
              dec_ref, intra_ref, seg_ref, s0_ref,
              yf_ref, yb_ref, sout_ref, state_ref, *, n_chunks, lc):
    step = pl.program_id(2)
    @pl.when(step == 0)
    def _():
        state_ref[...] = s0_ref[...]

    lane = lax.broadcasted_iota(jnp.int32, (1, V7X_LANES), 1)
    lower32 = (lane % RET_QK_DIM) < (RET_QK_DIM // 2)
    head0 = lane < RET_QK_DIM
    blk = (lax.broadcasted_iota(jnp.int32, (V7X_LANES, V7X_LANES), 0) // RET_QK_DIM ==
           lax.broadcasted_iota(jnp.int32, (V7X_LANES, V7X_LANES), 1) // RET_QK_DIM)
    seg = seg_ref[...]

    def rot(x, cos, sin):
        partner = jnp.where(lower32, pltpu.roll(x, 96, 1), pltpu.roll(x, 32, 1))
        return x * cos + partner * sin

    fwd_refs = (qf_ref, kf_ref, vf_ref, gf_ref, cosf_ref, sinf_ref, yf_ref)
    bwd_refs = (qb_ref, kb_ref, vb_ref, gb_ref, cosb_ref, sinb_ref, yb_ref)
    work = ([(0, c, fwd_refs) for c in range(n_chunks)] + [(1, c, bwd_refs) for c in reversed(range(n_chunks))])
    heads = (head0, jnp.logical_not(head0))

    def scores(d, c, refs):
        q_ref, k_ref, v_ref, _, cos_ref, sin_ref, _ = refs
        rows = pl.ds(c * lc, lc)
        cos = cos_ref[rows, :]
        sin = sin_ref[rows, :]
        q = rot(q_ref[rows, :], cos, sin)
        k = rot(k_ref[rows, :], cos, sin) * (RET_QK_DIM ** -0.5)
        v = v_ref[rows, :]
        kb = k.astype(BF16)
        sc = [_dot_nt(jnp.where(sel, q, 0.0).astype(BF16), kb) for sel in heads]
        kv = _dot_tn((k * dec_ref[d, 1]).astype(BF16), v.astype(BF16))
        return q.astype(BF16), v, sc, jnp.where(blk, kv, 0.0)

    stage1 = [scores(*w) for w in work]

    states = []
    for d in range(2):
        s = state_ref[d]
        cdec = dec_ref[d, 2][0:1, :]
        for (wd, _, _), (_, _, _, kv) in zip(work, stage1):
            if wd == d:
                states.append(s)
                s = s * cdec + kv
        state_ref[d] = s

    outs = []
    for (d, _, _), (qb, v, sc, _), s in zip(work, stage1, states):
        o = _dot(qb, s.astype(BF16)) * dec_ref[d, 0]
        for hh, sel in enumerate(heads):
            o = o + _dot((sc[hh] * intra_ref[d, hh]).astype(BF16), jnp.where(sel, v, 0.0).astype(BF16))
        outs.append(o)
    cens = [o - _seg_dot(o, seg) * (1.0 / RET_QK_DIM) for o in outs]
    variances = [_seg_dot(cen * cen, seg) * (1.0 / RET_QK_DIM) for cen in cens]
    for (_, c, refs), cen, var in zip(work, cens, variances):
        g = refs[3][pl.ds(c * lc, lc), :]
        refs[6][pl.ds(c * lc, lc), :] = cen * lax.rsqrt(var + NORM_EPS) * (g * _sigmoid(g))

    @pl.when(step == pl.num_programs(2) - 1)
    def _():
        sout_ref[...] = state_ref[...]


def retention(z, cos, sin, dec, intra, seg, s0, tm):
    b, r, _ = z.shape
    nb = r // tm
    lc = dec.shape[3]
    fwd = lambda off: pl.BlockSpec((None, tm, V7X_LANES), lambda i, p, k: (i, k, off + p))
    bwd = lambda off: pl.BlockSpec((None, tm, V7X_LANES), lambda i, p, k: (i, nb - 1 - k, off + p))
    tabf = pl.BlockSpec((tm, V7X_LANES), lambda i, p, k: (k, 0))
    tabb = pl.BlockSpec((tm, V7X_LANES), lambda i, p, k: (nb - 1 - k, 0))
    st = pl.BlockSpec((None, None, 2, V7X_LANES, V7X_LANES), lambda i, p, k: (i, p, 0, 0, 0))
    ysd = jax.ShapeDtypeStruct((b, r, 2 * V7X_LANES), F32)
    return pl.pallas_call(
        functools.partial(_ret_body, n_chunks=tm // lc, lc=lc),
        grid=(b, 2, nb),
        in_specs=[fwd(COL_RQ), fwd(COL_RK), fwd(COL_RV), fwd(COL_RGF), tabf, tabf,
                  bwd(COL_RQ), bwd(COL_RK), bwd(COL_RV), bwd(COL_RGB), tabb, tabb,
                  pl.BlockSpec((None, 2, 3, lc, V7X_LANES), lambda i, p, k: (p, 0, 0, 0, 0)),
                  pl.BlockSpec((None, 2, 2, lc, lc), lambda i, p, k: (p, 0, 0, 0, 0)),
                  pl.BlockSpec((V7X_LANES, V7X_LANES), lambda i, p, k: (0, 0)),
                  st],
        out_specs=[pl.BlockSpec((None, tm, V7X_LANES), lambda i, p, k: (i, k, p)),
                   pl.BlockSpec((None, tm, V7X_LANES), lambda i, p, k: (i, nb - 1 - k, p)),
                   st],
        out_shape=[ysd, ysd, jax.ShapeDtypeStruct(s0.shape, F32)],
        scratch_shapes=[pltpu.VMEM((2, V7X_LANES, V7X_LANES), F32)],
        compiler_params=_cparams(("parallel", "parallel", "arbitrary"), 32),
        name="retention",
    )(z, z, z, z, cos, sin, z, z, z, z, cos, sin, dec, intra, seg, s0)


def _sg_body(su_ref, sv_ref, ng_ref, w_ref, bm_ref, o_ref):
    tm = su_ref.shape[0]
    u = jax.nn.gelu(su_ref[...])
    t = jax.nn.gelu(sv_ref[...])
    mu = jnp.mean(t, axis=-1, keepdims=True)
    cen = t - mu
    var = jnp.mean(cen * cen, axis=-1, keepdims=True)
    v = (cen * lax.rsqrt(var + NORM_EPS) * ng_ref[...]).astype(BF16)
    group = lax.broadcasted_iota(jnp.int32, (1, 2 * V7X_LANES), 1) // SG_GROUP_DIM
    bias = bm_ref[...]
    for c in range(tm // SG_CHUNK):
        vc = v[c * SG_CHUNK:(c + 1) * SG_CHUNK, :]
        s = bias
        for g in range(SG_GROUPS):
            s = s + jnp.where(group == g, _dot(w_ref[g], vc), 0.0)
        o_ref[c * SG_CHUNK:(c + 1) * SG_CHUNK, :] = (u[c * SG_CHUNK:(c + 1) * SG_CHUNK, :] * s).astype(BF16)


def spatial_gating(z, ng, w, bm, tm):
    b, r, _ = z.shape
    wide = 2 * V7X_LANES
    return pl.pallas_call(
        _sg_body,
        grid=(b, r // tm),
        in_specs=[pl.BlockSpec((None, tm, wide), lambda i, j: (i, j, COL_SU256)),
                  pl.BlockSpec((None, tm, wide), lambda i, j: (i, j, COL_SV256)),
                  pl.BlockSpec((1, wide), lambda i, j: (0, 0)),
                  pl.BlockSpec((SG_GROUPS, SG_CHUNK, SG_CHUNK), lambda i, j: (0, 0, 0)),
                  pl.BlockSpec((SG_CHUNK, wide), lambda i, j: (0, 0))],
        out_specs=pl.BlockSpec((None, tm, wide), lambda i, j: (i, j, 0)),
        out_shape=jax.ShapeDtypeStruct((b, r, wide), BF16),
        compiler_params=_cparams(("parallel", "parallel"), 32),
        name="spatial_gating",
    )(z, z, ng, w, bm)


def _out_proj_body(a_ref, yf_ref, yb_ref, sg_ref, x_ref, g1_ref, a2_ref, s2_ref, w_ref, rwt_ref,
                   xo_ref, f_ref, aff_ref):
    na = a_ref.shape[1]
    nr = yf_ref.shape[1]
    tm = x_ref.shape[0]
    part = min(tm, 256)
    parts = [slice(s, s + part) for s in range(0, tm, part)]
    ys = []
    for rows in parts:
        r = (yf_ref[rows, :] + yb_ref[rows, :]).astype(BF16)
        ys.append(_dot(a_ref[rows, :], w_ref[0:na, :]) + _dot(r, w_ref[na:na + nr, :])
                  + _dot(sg_ref[rows, :], w_ref[na + nr:, :]))
    for rows, y in zip(parts, ys):
        xn = x_ref[rows, :] + g1_ref[...] * y
        xo_ref[rows, :] = xn
        ms = jnp.mean(xn * xn, axis=-1, keepdims=True)
        f = (xn * lax.rsqrt(ms + NORM_EPS) * a2_ref[...] + s2_ref[...]).astype(BF16)
        f_ref[rows, :] = f
        logits = _dot_nt(rwt_ref[...], f)
        e = jnp.exp(logits - jnp.max(logits, axis=0, keepdims=True))
        aff_ref[:, rows] = e / jnp.sum(e, axis=0, keepdims=True)


def out_proj(a, yf, yb, sg, x, g1, a2, s2, w, rwt, tm):
    b, r, d = x.shape
    rows = lambda width: pl.BlockSpec((None, tm, width), lambda i, j: (i, j, 0))
    vec = pl.BlockSpec((None, 1, d), lambda i, j: (i, 0, 0))
    return pl.pallas_call(
        _out_proj_body,
        grid=(b, r // tm),
        in_specs=[rows(a.shape[2]), rows(yf.shape[2]), rows(yb.shape[2]), rows(sg.shape[2]), rows(d),
                  vec, vec, vec,
                  pl.BlockSpec(w.shape, lambda i, j: (0, 0)),
                  pl.BlockSpec(rwt.shape, lambda i, j: (0, 0))],
        out_specs=[rows(d), rows(d), pl.BlockSpec((None, N_EXPERTS, tm), lambda i, j: (i, 0, j))],
        out_shape=[jax.ShapeDtypeStruct((b, r, d), F32), jax.ShapeDtypeStruct((b, r, d), BF16),
                   jax.ShapeDtypeStruct((b, N_EXPERTS, r), F32)],
        compiler_params=_cparams(("parallel", "parallel"), 48),
        name="out_proj",
    )(a, yf, yb, sg, x, g1, a2, s2, w, rwt)


def _router_body(aff_ref, tri_ref, posm_ref, off_ref, *, cap, tb):
    aff = aff_ref[...]
    e, r = aff.shape
    nb = r // tb
    bits = pltpu.bitcast(aff, jnp.int32)

    def search(i, thr):
        cand = thr | lax.shift_left(jnp.int32(1), 30 - i)
        cnt = jnp.sum(jnp.where(bits >= cand, 1.0, 0.0), axis=1, keepdims=True)
        return jnp.where(cnt >= cap, cand, thr)

    thr = lax.fori_loop(0, 31, search, jnp.zeros((e, 1), jnp.int32))
    gt = jnp.where(bits > thr, 1.0, 0.0)
    eq = jnp.where(bits == thr, 1.0, 0.0)
    need = cap - jnp.sum(gt, axis=1, keepdims=True)
    tri = tri_ref[...]
    lane = lax.broadcasted_iota(jnp.int32, (e, V7X_LANES), 1)
    offs = jnp.zeros((e, V7X_LANES), jnp.int32)
    run_eq = jnp.zeros((e, 1), F32)
    run_sel = jnp.zeros((e, 1), F32)
    for j in range(nb):
        cols = slice(j * tb, (j + 1) * tb)
        eq_j = eq[:, cols]
        before_eq = _dot(eq_j.astype(BF16), tri) + run_eq
        sel_j = gt[:, cols] + eq_j * jnp.where(before_eq < need, 1.0, 0.0)
        pos_j = _dot(sel_j.astype(BF16), tri) + run_sel
        posm_ref[:, cols] = jnp.where(sel_j > 0.0, pos_j, -1.0)
        offs = jnp.where(lane == j, run_sel.astype(jnp.int32), offs)
        run_eq = run_eq + jnp.sum(eq_j, axis=1, keepdims=True)
        run_sel = run_sel + jnp.sum(sel_j, axis=1, keepdims=True)
    off_ref[...] = jnp.where(lane == nb, run_sel.astype(jnp.int32), offs)


def route(aff_t, tri, cap):
    b, e, r = aff_t.shape
    tb = tri.shape[0]
    posm, off = pl.pallas_call(
        functools.partial(_router_body, cap=cap, tb=tb),
        grid=(b,),
        in_specs=[pl.BlockSpec((None, e, r), lambda i: (i, 0, 0)),
                  pl.BlockSpec((tb, tb), lambda i: (0, 0))],
        out_specs=[pl.BlockSpec((None, e, r), lambda i: (i, 0, 0)),
                   pl.BlockSpec((None, e, V7X_LANES), lambda i: (i, 0, 0))],
        out_shape=[jax.ShapeDtypeStruct((b, e, r), F32), jax.ShapeDtypeStruct((b, e, V7X_LANES), jnp.int32)],
        compiler_params=_cparams(("parallel",), 32),
        name="router",
    )(aff_t, tri)
    shape5 = (b, e, r // tb, 1, tb)
    return posm.reshape(shape5), aff_t.reshape(shape5), off.reshape(-1)


def _gather_body(off_ref, f_ref, posm_ref, aff_ref, x_hbm, g_hbm, xall, gall, pcat, sem, *, cap, w, nb, n_exp):
    b = pl.program_id(0)
    j = pl.program_id(1)
    align = V7X_BF16_SUBLANES

    @pl.when(j == 0)
    def _():
        xall[...] = jnp.zeros(xall.shape, BF16)
        gall[...] = jnp.zeros(gall.shape, F32)

    iota = lax.broadcasted_iota(jnp.int32, (w, 1), 0).astype(F32)

    def window(e, rnd):
        off = off_ref[(b * n_exp + e) * V7X_LANES + j]
        return pl.multiple_of(jnp.minimum((off // align) * align + rnd * w, cap), align)

    def one_round(rnd):
        for e in range(n_exp):
            first = window(e, rnd)
            onehot = posm_ref[e] == first.astype(F32) + iota
            pcat[e * w:(e + 1) * w, :] = jnp.where(onehot, 1.0, 0.0).astype(BF16)
            gall[e, pl.ds(first, w), :] += jnp.sum(jnp.where(onehot, aff_ref[e], 0.0), axis=1, keepdims=True)
        rows = _dot(pcat[...], f_ref[...])
        for e in range(n_exp):
            xall[e, pl.ds(window(e, rnd), w), :] += rows[e * w:(e + 1) * w, :].astype(BF16)

    one_round(0)
    span = jnp.int32(0)
    for e in range(n_exp):
        idx = (b * n_exp + e) * V7X_LANES + j
        span = jnp.maximum(span, off_ref[idx + 1] - (off_ref[idx] // align) * align)

    def extra(rnd, carry):
        one_round(rnd)
        return carry

    lax.fori_loop(1, (span + w - 1) // w, extra, 0)

    @pl.when(j == nb - 1)
    def _():
        cx = pltpu.make_async_copy(xall.at[:, pl.ds(0, cap), :], x_hbm.at[b], sem.at[0])
        cg = pltpu.make_async_copy(gall.at[:, pl.ds(0, cap), :], g_hbm.at[b], sem.at[1])
        cx.start()
        cg.start()
        cx.wait()
        cg.wait()


def moe_gather(off, f, posm, aff, cap, w):
    b, r, d = f.shape
    _, e, nb, _, tb = posm.shape
    blk = pl.BlockSpec((None, e, None, 1, tb), lambda i, j, o: (i, 0, j, 0, 0))
    return pl.pallas_call(
        functools.partial(_gather_body, cap=cap, w=w, nb=nb, n_exp=e),
        grid_spec=pltpu.PrefetchScalarGridSpec(
            num_scalar_prefetch=1,
            grid=(b, nb),
            in_specs=[pl.BlockSpec((None, tb, d), lambda i, j, o: (i, j, 0)), blk, blk],
            out_specs=[pl.BlockSpec(memory_space=pl.ANY), pl.BlockSpec(memory_space=pl.ANY)],
            scratch_shapes=[pltpu.VMEM((e, cap + w, d), BF16), pltpu.VMEM((e, cap + w, 1), F32),
                            pltpu.VMEM((e * w, tb), BF16), pltpu.SemaphoreType.DMA((2,))]),
        out_shape=[jax.ShapeDtypeStruct((b, e, cap, d), BF16), jax.ShapeDtypeStruct((b, e, cap, 1), F32)],
        compiler_params=_cparams(("arbitrary", "arbitrary"), 56),
        name="moe_gather",
    )(off, f, posm, aff)


def _moe_ffn_body(*refs, has_ctx, cap, capc):
    if has_ctx:
        _, x_ref, g_ref, xc_ref, gc_ref, wg_ref, wu_ref, wd_ref, y_ref, yc_ref, yacc, ycacc = refs
    else:
        _, x_ref, g_ref, wg_ref, wu_ref, wd_ref, y_ref, yacc = refs
    c = pl.program_id(1)
    last = pl.num_programs(1) - 1
    nbatch, _, d = x_ref.shape
    wg = wg_ref[...].astype(BF16)
    wu = wu_ref[...].astype(BF16)
    wd = wd_ref[...].astype(BF16)

    def hidden(x):
        a = _dot(x, wg)
        u = _dot(x, wu)
        return (a * _sigmoid(a) * u).astype(BF16)

    @pl.when(c == 0)
    def _():
        yacc[...] = jnp.zeros(yacc.shape, F32)
        if has_ctx:
            ycacc[...] = jnp.zeros(ycacc.shape, F32)

    hs = [hidden(x_ref[i]) for i in range(nbatch)]
    if has_ctx:
        hc = hidden(xc_ref[...].reshape(nbatch * capc, d))
    for i in range(nbatch):
        yacc[i] += _dot(hs[i], wd)
    if has_ctx:
        ycacc[...] += _dot(hc, wd)

    @pl.when(c == last)
    def _():
        for i in range(nbatch):
            y_ref[i, 0:cap, :] = (yacc[i] * g_ref[i]).astype(BF16)
            y_ref[i, cap:, :] = jnp.zeros((y_ref.shape[1] - cap, d), BF16)
        if has_ctx:
            yc = ycacc[...] * gc_ref[...].reshape(nbatch * capc, 1)
            yc_ref[:, 0:capc, :] = yc.reshape(nbatch, capc, d).astype(BF16)
            yc_ref[:, capc:, :] = jnp.zeros((nbatch, yc_ref.shape[1] - capc, d), BF16)


def moe_ffn(layer, x, g, xc, gc, wg, wu, wd, tf, pad, padc):
    b, e, cap, d = x.shape
    ff = wg.shape[3]
    has_ctx = xc is not None
    capc = xc.shape[2] if has_ctx else 0
    tok = lambda rows, width: pl.BlockSpec((b, None, rows, width), lambda j, k, l: (0, j, 0, 0))
    in_specs = [tok(cap, d), tok(cap, 1)]
    out_specs = [tok(cap + pad, d)]
    out_shape = [jax.ShapeDtypeStruct((b, e, cap + pad, d), BF16)]
    scratch = [pltpu.VMEM((b, cap, d), F32)]
    args = [x, g]
    if has_ctx:
        in_specs += [tok(capc, d), tok(capc, 1)]
        out_specs.append(tok(capc + padc, d))
        out_shape.append(jax.ShapeDtypeStruct((b, e, capc + padc, d), BF16))
        scratch.append(pltpu.VMEM((b * capc, d), F32))
        args += [xc, gc]
    in_specs += [pl.BlockSpec((None, None, d, tf), lambda j, k, l: (l[0], j, 0, k)),
                 pl.BlockSpec((None, None, d, tf), lambda j, k, l: (l[0], j, 0, k)),
                 pl.BlockSpec((None, None, tf, d), lambda j, k, l: (l[0], j, k, 0))]
    out = pl.pallas_call(
        functools.partial(_moe_ffn_body, has_ctx=has_ctx, cap=cap, capc=capc),
        grid_spec=pltpu.PrefetchScalarGridSpec(
            num_scalar_prefetch=1, grid=(e, ff // tf),
            in_specs=in_specs, out_specs=out_specs, scratch_shapes=scratch),
        out_shape=out_shape,
        compiler_params=_cparams(("parallel", "arbitrary"), 56),
        name="moe_ffn",
    )(layer, *args, wg, wu, wd)
    return out if has_ctx else (out[0], None)


def _scatter_body(off_ref, x_ref, g2_ref, posm_ref, y_hbm, xo_ref, ybuf, pcat, sem, *, cap, w, nb, n_exp):
    b = pl.program_id(0)
    j = pl.program_id(1)
    step = b * nb + j
    slot = lax.rem(step, 2)
    spare = 2
    align = V7X_BF16_SUBLANES

    def window(bb, jj, e, rnd):
        off = off_ref[(bb * n_exp + e) * V7X_LANES + jj]
        return jnp.minimum((off // align) * align + rnd * w, cap)

    def copies(bb, jj, buf, rnd):
        return [pltpu.make_async_copy(
            y_hbm.at[bb, e, pl.ds(pl.multiple_of(window(bb, jj, e, rnd), align), w), :],
            ybuf.at[buf, pl.ds(e * w, w), :], sem.at[buf]) for e in range(n_exp)]

    @pl.when(step == 0)
    def _():
        for cp in copies(b, j, 0, 0):
            cp.start()

    @pl.when(step + 1 < pl.num_programs(0) * nb)
    def _():
        nxt = step + 1
        for cp in copies(nxt // nb, lax.rem(nxt, nb), 1 - slot, 0):
            cp.start()

    iota = lax.broadcasted_iota(jnp.int32, (w, 1), 0).astype(F32)

    def onehots(rnd):
        for e in range(n_exp):
            first = window(b, j, e, rnd).astype(F32)
            pcat[e * w:(e + 1) * w, :] = jnp.where(posm_ref[e] == first + iota, 1.0, 0.0).astype(BF16)

    for cp in copies(b, j, slot, 0):
        cp.wait()
    onehots(0)
    g2 = g2_ref[...]
    xo_ref[...] = x_ref[...] + g2 * _dot_tn(pcat[...], ybuf[slot])

    span = jnp.int32(0)
    for e in range(n_exp):
        idx = (b * n_exp + e) * V7X_LANES + j
        span = jnp.maximum(span, off_ref[idx + 1] - (off_ref[idx] // align) * align)

    def extra(rnd, carry):
        for cp in copies(b, j, spare, rnd):
            cp.start()
        for cp in copies(b, j, spare, rnd):
            cp.wait()
        onehots(rnd)
        xo_ref[...] += g2 * _dot_tn(pcat[...], ybuf[spare])
        return carry

    lax.fori_loop(1, (span + w - 1) // w, extra, 0)


def moe_scatter(off, x, g2, posm, y, cap, w):
    b, r, d = x.shape
    _, e, nb, _, tb = posm.shape
    rows = pl.BlockSpec((None, tb, d), lambda i, j, o: (i, j, 0))
    return pl.pallas_call(
        functools.partial(_scatter_body, cap=cap, w=w, nb=nb, n_exp=e),
        grid_spec=pltpu.PrefetchScalarGridSpec(
            num_scalar_prefetch=1,
            grid=(b, nb),
            in_specs=[rows, pl.BlockSpec((None, 1, d), lambda i, j, o: (i, 0, 0)),
                      pl.BlockSpec((None, e, None, 1, tb), lambda i, j, o: (i, 0, j, 0, 0)),
                      pl.BlockSpec(memory_space=pl.ANY)],
            out_specs=rows,
            scratch_shapes=[pltpu.VMEM((3, e * w, d), BF16), pltpu.VMEM((e * w, tb), BF16),
                            pltpu.SemaphoreType.DMA((3,))]),
        out_shape=jax.ShapeDtypeStruct((b, r, d), F32),
        compiler_params=_cparams(("arbitrary", "arbitrary"), 32),
        name="moe_scatter",
    )(off, x, g2, posm, y)


def _attn_tables(n):
    rows = n // GRID_W
    row = jnp.broadcast_to(jnp.arange(rows, dtype=F32)[:, None], (rows, GRID_W)).reshape(-1)
    col = jnp.broadcast_to(jnp.arange(GRID_W, dtype=F32)[None, :], (rows, GRID_W)).reshape(-1)
    quarter = DA_HEAD_DIM // 4
    inv = ROPE_BASE ** (-jnp.arange(quarter, dtype=F32) / quarter)
    ang_r, ang_c = row[:, None] * inv, col[:, None] * inv

    def half(ang):
        return (jnp.concatenate([jnp.cos(ang), jnp.cos(ang)], -1),
                jnp.concatenate([-jnp.sin(ang), jnp.sin(ang)], -1))

    cr, sr = half(ang_r)
    cc, sc = half(ang_c)
    cos64 = jnp.concatenate([cr, cc], -1)
    sin64 = jnp.concatenate([sr, sc], -1)
    return jnp.tile(cos64, (1, 2)), jnp.tile(sin64, (1, 2))


def _ret_tables(n):
    inv = 1.0 / (ROPE_BASE ** jnp.linspace(0.0, 1.0, RET_QK_DIM // 2, dtype=F32))
    ang = jnp.arange(n, dtype=F32)[:, None] * inv
    cos64 = jnp.concatenate([jnp.cos(ang), jnp.cos(ang)], -1)
    sin64 = jnp.concatenate([-jnp.sin(ang), jnp.sin(ang)], -1)
    return jnp.tile(cos64, (1, 2)), jnp.tile(sin64, (1, 2))


def _ret_decay_tables(log_gamma, lc):
    pos = jnp.arange(lc, dtype=F32)
    lg = jnp.repeat(log_gamma, RET_QK_DIM, axis=1).reshape(2, 2, 1, V7X_LANES)
    qd = jnp.stack([jnp.exp(lg[0] * (pos + 1.0)[None, :, None]), jnp.exp(lg[1] * (lc - pos)[None, :, None])])
    kd = jnp.stack([jnp.exp(lg[0] * (lc - 1.0 - pos)[None, :, None]), jnp.exp(lg[1] * pos[None, :, None])])
    cd = jnp.broadcast_to(jnp.exp(lg * lc), (2, 2, lc, V7X_LANES))
    dec = jnp.stack([qd, kd, cd], axis=2)
    dist = pos[:, None] - pos[None, :]
    lgh = log_gamma[:, :, None, None]
    fwd = jnp.where(dist >= 0, jnp.exp(lgh[0] * jnp.maximum(dist, 0.0)), 0.0)
    bwd = jnp.where(dist <= 0, jnp.exp(lgh[1] * jnp.maximum(-dist, 0.0)), 0.0)
    intra = jnp.stack([fwd, bwd]).reshape(2, 2, 2, lc, lc)
    return jnp.swapaxes(dec, 0, 1), jnp.swapaxes(intra, 0, 1)


def kernel(x, c, ctx, c_ctx, w_mod, b_mod, norm1_g, norm2_g, w_in, w_out, da_q_norm_g, da_k_norm_g,
           da_lambda_q1, da_lambda_k1, da_lambda_q2, da_lambda_k2, da_subln_g, ret_decay,
           sg_norm_g, sg_w, sg_b, router_w, ex_w_gate, ex_w_up, ex_w_down):
    b, n, d = x.shape
    n_ctx = ctx.shape[1]
    cos_a, sin_a = _attn_tables(n)
    cos_r, sin_r = _ret_tables(n)
    one_c = jnp.ones((n_ctx, V7X_LANES), F32)
    zero_c = jnp.zeros((n_ctx, V7X_LANES), F32)
    lane = jnp.arange(V7X_LANES)
    seg = (lane[:, None] // 64 == lane[None, :] // 64).astype(BF16)
    state0 = jnp.zeros((b, 2, 2, V7X_LANES, V7X_LANES), F32)
    tk = n_ctx
    assert (n + n_ctx) // tk % ATTN_BLOCKS_PER_STEP == 0
    tok = jnp.arange(MOE_TOKEN_BLOCK)
    tri = (tok[:, None] < tok[None, :]).astype(BF16)
    cap = EC_CAPACITY * n // N_EXPERTS
    cap_c = EC_CAPACITY * n_ctx // N_EXPERTS
    pad, pad_c = min(64, cap), min(64, cap_c)

    for layer in range(DEPTH):
        need_ctx = layer < DEPTH - 1
        lambda_init = 0.8 - 0.6 * math.exp(-0.3 * layer)
        mod = jax.nn.silu(c) @ w_mod[layer] + b_mod[layer]
        mod_c = jnp.broadcast_to(jax.nn.silu(c_ctx) @ w_mod[layer] + b_mod[layer], mod.shape)
        sh1, sc1, g1, sh2, sc2, g2 = jnp.split(mod[:, None, :], 6, axis=-1)
        csh1, csc1, cg1, csh2, csc2, cg2 = jnp.split(mod_c[:, None, :], 6, axis=-1)
        lam = (jnp.exp(jnp.sum(da_lambda_q1[layer] * da_lambda_k1[layer]))
               - jnp.exp(jnp.sum(da_lambda_q2[layer] * da_lambda_k2[layer])) + lambda_init).reshape(1)
        log_gamma = jax.nn.log_sigmoid(ret_decay[layer].astype(F32))
        dec, intra = _ret_decay_tables(log_gamma, RET_SCAN_CHUNK)
        w_in_b = w_in[layer].astype(BF16)
        w_out_b = w_out[layer].astype(BF16)
        rwt = router_w[layer].T.astype(BF16)
        qg = jnp.tile(da_q_norm_g[layer], 2)[None, :]
        kg = jnp.tile(da_k_norm_g[layer], 2)[None, :]
        gcol = (da_subln_g[layer] * (1.0 - lambda_init))[:, None]
        sg_ng = sg_norm_g[layer][None, :]
        sg_wb = sg_w[layer].astype(BF16)
        sg_bm = jnp.repeat(sg_b[layer].T, SG_GROUP_DIM, axis=1)
        n1, n2 = norm1_g[layer][None, None, :], norm2_g[layer][None, None, :]

        z_lat = in_proj(x, n1 * (1.0 + sc1), sh1, w_in_b, 512)
        z_ctx = in_proj(ctx, n1 * (1.0 + csc1), csh1, w_in_b, n_ctx)

        cq1, cq2, ck, cvt = qkv_prep(z_ctx, one_c, zero_c, qg, kg, seg, n_ctx, tk)
        q1, q2, kl, vlt = qkv_prep(z_lat, cos_a, sin_a, qg, kg, seg, _row_tile(n, 1024), tk)
        k_all = jnp.concatenate([ck, kl], axis=2)
        vt_all = jnp.concatenate([cvt, vlt], axis=2)
        bound = score_bound(da_q_norm_g[layer], da_k_norm_g[layer])
        c1, c2 = softmax_shifts(bound, q1, q2, k_all, ATTN_QUERY_TILE, tk)
        a_lat = diff_attention(lam, q1, q2, c1, c2, k_all, vt_all, gcol, ATTN_QUERY_TILE, ATTN_BLOCKS_PER_STEP)

        cyf, cyb, cstate = retention(z_ctx, one_c, zero_c, dec, intra, seg, state0, n_ctx)
        yf, yb, _ = retention(z_lat, cos_r, sin_r, dec, intra, seg, cstate, _row_tile(n, 1024))

        s_lat = spatial_gating(z_lat, sg_ng, sg_wb, sg_bm, 512)

        x, f_lat, aff_lat = out_proj(a_lat, yf, yb, s_lat, x, g1, n2 * (1.0 + sc2), sh2, w_out_b, rwt,
                                     _row_tile(n, 1024))
        posm, aff5, off = route(aff_lat, tri, cap)
        xs, gs = moe_gather(off, f_lat, posm, aff5, cap, pad)
        xcs = gcs = None
        if need_ctx:
            cc1, cc2 = softmax_shifts(bound, cq1, cq2, ck, n_ctx, tk)
            a_ctx = diff_attention(lam, cq1, cq2, cc1, cc2, ck, cvt, gcol, n_ctx, 1)
            s_ctx = spatial_gating(z_ctx, sg_ng, sg_wb, sg_bm, n_ctx)
            ctx, f_ctx, aff_ctx = out_proj(a_ctx, cyf, cyb, s_ctx, ctx, cg1, n2 * (1.0 + csc2), csh2,
                                           w_out_b, rwt, n_ctx)
            posm_c, aff5_c, off_c = route(aff_ctx, tri, cap_c)
            xcs, gcs = moe_gather(off_c, f_ctx, posm_c, aff5_c, cap_c, pad_c)
        ys, ycs = moe_ffn(jnp.full((1,), layer, jnp.int32), xs, gs, xcs, gcs, ex_w_gate, ex_w_up, ex_w_down,
                          512, pad, pad_c)
        x = moe_scatter(off, x, g2, posm, ys, cap, pad)
        if need_ctx:
            ctx = moe_scatter(off_c, ctx, cg2, posm_c, ycs, cap_c, pad_c)
    return x
```

```python
import functools
import math

import jax
import jax.numpy as jnp
from jax import lax
from jax.experimental import pallas as pl
from jax.experimental.pallas import tpu as pltpu

F32 = jnp.float32
BF16 = jnp.bfloat16

D_MODEL = 1024
DEPTH = 4
GRID_W = 64
NORM_EPS = 1e-6
ROPE_BASE = 10000.0
DA_HEADS = 4
DA_HEAD_DIM = 64
RET_HEADS = 4
RET_QK_DIM = 64
RET_SCAN_CHUNK = 128
SG_GROUPS = 4
SG_GROUP_DIM = 64
SG_CHUNK = 128
N_EXPERTS = 16
EXPERT_FF = 2 * D_MODEL
EC_CAPACITY = 2
IN_COLS = 3328
COL_AQ, COL_AK, COL_AV = 0, 4, 8
COL_RQ, COL_RK, COL_RV, COL_RGF, COL_RGB = 12, 14, 16, 18, 20
COL_SU256, COL_SV256 = 11, 12

V7X_LANES = 128
V7X_BF16_SUBLANES = 16
V7X_VMEM_BYTES = 64 * 1024 * 1024
MOE_TOKEN_BLOCK = 256
ATTN_BLOCKS_PER_STEP = 11
ATTN_QUERY_TILE = 512
NEG_BIG = -1e30
LOG2_E = 1.4426950408889634
F32_EXP2_RANGE = 120.0


def _cparams(semantics, vmem_mib):
    assert vmem_mib * 1024 * 1024 < V7X_VMEM_BYTES
    return pltpu.CompilerParams(dimension_semantics=semantics, vmem_limit_bytes=vmem_mib * 1024 * 1024)


def _row_tile(rows, preferred):
    while rows % preferred:
        preferred //= 2
    return preferred


def _dot(a, b):
    return jnp.dot(a, b, preferred_element_type=F32)


def _dot_nt(a, b):
    return lax.dot_general(a, b, (((1,), (1,)), ((), ())), preferred_element_type=F32)


def _dot_tn(a, b):
    return lax.dot_general(a, b, (((0,), (0,)), ((), ())), preferred_element_type=F32)


def _seg_dot(x, seg):
    hi = x.astype(BF16)
    lo = (x - hi.astype(F32)).astype(BF16)
    return _dot(hi, seg) + _dot(lo, seg)


def _sigmoid(x):
    return 1.0 / (1.0 + jnp.exp(-x))


def _in_proj_body(x_ref, a_ref, s_ref, w_ref, z_ref):
    x = x_ref[...]
    ms = jnp.mean(x * x, axis=-1, keepdims=True)
    h = x * lax.rsqrt(ms + NORM_EPS) * a_ref[...] + s_ref[...]
    z_ref[...] = _dot(h.astype(BF16), w_ref[...])


def in_proj(x, a, s, w, tm):
    b, r, d = x.shape
    c = w.shape[1]
    return pl.pallas_call(
        _in_proj_body,
        grid=(b, r // tm),
        in_specs=[
            pl.BlockSpec((None, tm, d), lambda i, j: (i, j, 0)),
            pl.BlockSpec((None, 1, d), lambda i, j: (i, 0, 0)),
            pl.BlockSpec((None, 1, d), lambda i, j: (i, 0, 0)),
            pl.BlockSpec((d, c), lambda i, j: (0, 0)),
        ],
        out_specs=pl.BlockSpec((None, tm, c), lambda i, j: (i, j, 0)),
        out_shape=jax.ShapeDtypeStruct((b, r, c), F32),
        compiler_params=_cparams(("parallel", "parallel"), 48),
        name="in_proj",
    )(x, a, s, w)


def _qkv_prep_body(q_ref, k_ref, v_ref, cos_ref, sin_ref, qg_ref, kg_ref, seg_ref,
                   qp1_ref, qp2_ref, kk_ref, vt_ref):
    lane = lax.broadcasted_iota(jnp.int32, (1, V7X_LANES), 1)
    lower16 = (lane % 32) < 16
    seg = seg_ref[...]
    cos = cos_ref[...]
    sin = sin_ref[...]

    def norm_rope(x, g):
        ms = _seg_dot(x * x, seg) * (1.0 / DA_HEAD_DIM)
        y = x * lax.rsqrt(ms + NORM_EPS) * g
        partner = jnp.where(lower16, pltpu.roll(y, 112, 1), pltpu.roll(y, 16, 1))
        return y * cos + partner * sin

    q = norm_rope(q_ref[...], qg_ref[...]) * (DA_HEAD_DIM ** -0.5 * LOG2_E)
    first = lane < DA_HEAD_DIM
    qp1_ref[...] = jnp.where(first, q, 0.0).astype(BF16)
    qp2_ref[...] = jnp.where(first, 0.0, q).astype(BF16)
    kk_ref[...] = norm_rope(k_ref[...], kg_ref[...]).astype(BF16)
    tv = vt_ref.shape[2]
    for s in range(vt_ref.shape[0]):
        vt_ref[s] = v_ref[s * tv:(s + 1) * tv, :].T.astype(BF16)


def qkv_prep(z, cos, sin, qg, kg, seg, tm, tv):
    b, r, _ = z.shape
    h = DA_HEADS
    row = lambda off: pl.BlockSpec((None, tm, V7X_LANES), lambda i, j, k: (i, k, off + j))
    tab = pl.BlockSpec((tm, V7X_LANES), lambda i, j, k: (k, 0))
    vec = pl.BlockSpec((1, V7X_LANES), lambda i, j, k: (0, 0))
    o4 = pl.BlockSpec((None, None, tm, V7X_LANES), lambda i, j, k: (i, j, k, 0))
    sd = jax.ShapeDtypeStruct((b, h, r, V7X_LANES), BF16)
    return pl.pallas_call(
        _qkv_prep_body,
        grid=(b, h, r // tm),
        in_specs=[row(COL_AQ), row(COL_AK), row(COL_AV), tab, tab, vec, vec,
                  pl.BlockSpec((V7X_LANES, V7X_LANES), lambda i, j, k: (0, 0))],
        out_specs=[o4, o4, o4,
                   pl.BlockSpec((None, None, tm // tv, V7X_LANES, tv), lambda i, j, k: (i, j, k, 0, 0))],
        out_shape=[sd, sd, sd, jax.ShapeDtypeStruct((b, h, r // tv, V7X_LANES, tv), BF16)],
        compiler_params=_cparams(("parallel", "parallel", "parallel"), 32),
        name="qkv_prep",
    )(z, z, z, cos, sin, qg, kg, seg)


def _attn_body(*refs, bps, n_src):
    lam_ref, q1_ref, q2_ref, c1_ref, c2_ref = refs[:5]
    srcs = [(refs[5 + 2 * s], refs[6 + 2 * s]) for s in range(n_src)]
    g_ref, o_ref, acc_ref = refs[5 + 2 * n_src:]
    q1 = q1_ref[...]
    q2 = q2_ref[...]
    c1 = c1_ref[...]
    c2 = c2_ref[...]
    tq = q1.shape[0]
    tk = srcs[0][1].shape[2]
    blocks = [(s, j) for s, (_, v_ref) in enumerate(srcs) for j in range(v_ref.shape[0])]
    n_steps = len(blocks) // bps

    def runs(i):
        out = []
        for s, j in blocks[i * bps:(i + 1) * bps]:
            if out and out[-1][0] == s and out[-1][2] == j:
                out[-1][2] = j + 1
            else:
                out.append([s, j, j + 1])
        return out

    def probs(i):
        segs, d1, d2 = [], 0.0, 0.0
        for s, lo, hi in runs(i):
            k = srcs[s][0][lo * tk:hi * tk, :]
            p1 = jnp.exp2(_dot_nt(k, q1) - c1)
            p2 = jnp.exp2(_dot_nt(k, q2) - c2)
            segs.append((s, lo, hi, (p1.astype(BF16), p2.astype(BF16))))
            d1 = d1 + jnp.sum(p1.reshape(-1, 8, tq), axis=0)
            d2 = d2 + jnp.sum(p2.reshape(-1, 8, tq), axis=0)
        return segs, d1, d2

    def accumulate(segs):
        for idx in range(2):
            part = None
            for s, lo, hi, ps in segs:
                for j in range(lo, hi):
                    term = _dot(srcs[s][1][j], ps[idx][(j - lo) * tk:(j - lo + 1) * tk])
                    part = term if part is None else part + term
            acc_ref[idx] += part

    acc_ref[...] = jnp.zeros(acc_ref.shape, F32)
    segs, l1, l2 = probs(0)
    for i in range(n_steps - 1):
        nxt, d1, d2 = probs(i + 1)
        accumulate(segs)
        segs, l1, l2 = nxt, l1 + d1, l2 + d2
    accumulate(segs)
    l1 = jnp.sum(l1, axis=0, keepdims=True)
    l2 = jnp.sum(l2, axis=0, keepdims=True)
    o = acc_ref[0] / l1 - lam_ref[0] * (acc_ref[1] / l2)
    ms = jnp.mean(o * o, axis=0, keepdims=True)
    y = o * lax.rsqrt(ms + NORM_EPS) * g_ref[...]
    o_ref[...] = y.T.astype(BF16)


def diff_attention(lam, q1, q2, c1, c2, sources, gcol, tq, bps):
    b, h, nq, _ = q1.shape
    qs = pl.BlockSpec((None, None, tq, V7X_LANES), lambda i, j, s: (i, j, s, 0))
    cs = pl.BlockSpec((None, None, 1, tq), lambda i, j, s: (i, j, 0, s))
    in_specs = [pl.BlockSpec(memory_space=pltpu.SMEM), qs, qs, cs, cs]
    args = [lam, q1, q2, c1, c2]
    for k, vt in sources:
        in_specs += [pl.BlockSpec((None, None) + k.shape[2:], lambda i, j, s: (i, j, 0, 0)),
                     pl.BlockSpec((None, None) + vt.shape[2:], lambda i, j, s: (i, j, 0, 0, 0))]
        args += [k, vt]
    assert sum(vt.shape[2] for _, vt in sources) % bps == 0
    return pl.pallas_call(
        functools.partial(_attn_body, bps=bps, n_src=len(sources)),
        grid=(b, h, nq // tq),
        in_specs=in_specs + [pl.BlockSpec((V7X_LANES, 1), lambda i, j, s: (0, 0))],
        out_specs=pl.BlockSpec((None, tq, V7X_LANES), lambda i, j, s: (i, s, j)),
        out_shape=jax.ShapeDtypeStruct((b, nq, h * V7X_LANES), BF16),
        scratch_shapes=[pltpu.VMEM((2, V7X_LANES, tq), F32)],
        compiler_params=_cparams(("parallel", "parallel", "parallel"), 48),
        name="diff_attention",
    )(*args, gcol)


def _rowmax_body(q1_ref, q2_ref, k_ref, m1_ref, m2_ref, *, n_blocks, tk):
    q1 = q1_ref[...]
    q2 = q2_ref[...]

    def body(i, carry):
        k = k_ref[pl.ds(pl.multiple_of(i * tk, tk), tk), :]
        return (jnp.maximum(carry[0], jnp.max(_dot_nt(k, q1), axis=0, keepdims=True)),
                jnp.maximum(carry[1], jnp.max(_dot_nt(k, q2), axis=0, keepdims=True)))

    start = jnp.full((1, q1.shape[0]), NEG_BIG, F32)
    m1, m2 = lax.fori_loop(0, n_blocks, body, (start, start))
    m1_ref[...] = m1
    m2_ref[...] = m2


def score_rowmax(q1, q2, k, tq, tk):
    b, h, nq, _ = q1.shape
    nk = k.shape[2]
    qs = pl.BlockSpec((None, None, tq, V7X_LANES), lambda i, j, s: (i, j, s, 0))
    ms = pl.BlockSpec((None, None, 1, tq), lambda i, j, s: (i, j, 0, s))
    sd = jax.ShapeDtypeStruct((b, h, 1, nq), F32)
    return pl.pallas_call(
        functools.partial(_rowmax_body, n_blocks=nk // tk, tk=tk),
        grid=(b, h, nq // tq),
        in_specs=[qs, qs, pl.BlockSpec((None, None, nk, V7X_LANES), lambda i, j, s: (i, j, 0, 0))],
        out_specs=[ms, ms],
        out_shape=[sd, sd],
        compiler_params=_cparams(("parallel", "parallel", "parallel"), 32),
        name="score_rowmax",
    )(q1, q2, k)


def score_bound(qg, kg):
    return 1.01 * DA_HEAD_DIM ** 0.5 * LOG2_E * jnp.max(jnp.abs(qg)) * jnp.max(jnp.abs(kg))


def softmax_shifts(bound, q1, q2, keys, tq, tk):
    shape = q1.shape[:2] + (1, q1.shape[2])
    flat = lambda: (jnp.full(shape, bound, F32),) * 2
    exact = lambda: tuple(score_rowmax(q1, q2, jnp.concatenate(keys, axis=2), tq, tk))
    return lax.cond(2.0 * bound <= F32_EXP2_RANGE, flat, exact)


def _ret_body(qf_ref, kf_ref, vf_ref, gf_ref, cosf_ref, sinf_ref,
              qb_ref, kb_ref, vb_ref, gb_ref, cosb_ref, sinb_ref,
              dec_ref, intra_ref, seg_ref, s0_ref,
              yf_ref, yb_ref, sout_ref, state_ref, *, n_chunks, lc):
    step = pl.program_id(2)
    @pl.when(step == 0)
    def _():
        state_ref[...] = s0_ref[...]

    lane = lax.broadcasted_iota(jnp.int32, (1, V7X_LANES), 1)
    lower32 = (lane % RET_QK_DIM) < (RET_QK_DIM // 2)
    head0 = lane < RET_QK_DIM
    blk = (lax.broadcasted_iota(jnp.int32, (V7X_LANES, V7X_LANES), 0) // RET_QK_DIM ==
           lax.broadcasted_iota(jnp.int32, (V7X_LANES, V7X_LANES), 1) // RET_QK_DIM)
    seg = seg_ref[...]

    def rot(x, cos, sin):
        partner = jnp.where(lower32, pltpu.roll(x, 96, 1), pltpu.roll(x, 32, 1))
        return x * cos + partner * sin

    fwd_refs = (qf_ref, kf_ref, vf_ref, gf_ref, cosf_ref, sinf_ref, yf_ref)
    bwd_refs = (qb_ref, kb_ref, vb_ref, gb_ref, cosb_ref, sinb_ref, yb_ref)
    work = ([(0, c, fwd_refs) for c in range(n_chunks)] + [(1, c, bwd_refs) for c in reversed(range(n_chunks))])
    heads = (head0, jnp.logical_not(head0))

    def scores(d, c, refs):
        q_ref, k_ref, v_ref, _, cos_ref, sin_ref, _ = refs
        rows = pl.ds(c * lc, lc)
        cos = cos_ref[rows, :]
        sin = sin_ref[rows, :]
        q = rot(q_ref[rows, :], cos, sin)
        k = rot(k_ref[rows, :], cos, sin) * (RET_QK_DIM ** -0.5)
        v = v_ref[rows, :]
        kb = k.astype(BF16)
        sc = [_dot_nt(jnp.where(sel, q, 0.0).astype(BF16), kb) for sel in heads]
        kv = _dot_tn((k * dec_ref[d, 1]).astype(BF16), v.astype(BF16))
        return q.astype(BF16), v, sc, jnp.where(blk, kv, 0.0)

    stage1 = [scores(*w) for w in work]

    states = []
    for d in range(2):
        s = state_ref[d]
        cdec = dec_ref[d, 2][0:1, :]
        for (wd, _, _), (_, _, _, kv) in zip(work, stage1):
            if wd == d:
                states.append(s)
                s = s * cdec + kv
        state_ref[d] = s

    outs = []
    for (d, _, _), (qb, v, sc, _), s in zip(work, stage1, states):
        o = _dot(qb, s.astype(BF16)) * dec_ref[d, 0]
        for hh, sel in enumerate(heads):
            o = o + _dot((sc[hh] * intra_ref[d, hh]).astype(BF16), jnp.where(sel, v, 0.0).astype(BF16))
        outs.append(o)
    cens = [o - _seg_dot(o, seg) * (1.0 / RET_QK_DIM) for o in outs]
    variances = [_seg_dot(cen * cen, seg) * (1.0 / RET_QK_DIM) for cen in cens]
    for (_, c, refs), cen, var in zip(work, cens, variances):
        g = refs[3][pl.ds(c * lc, lc), :]
        refs[6][pl.ds(c * lc, lc), :] = cen * lax.rsqrt(var + NORM_EPS) * (g * _sigmoid(g))

    @pl.when(step == pl.num_programs(2) - 1)
    def _():
        sout_ref[...] = state_ref[...]


def retention(z, cos, sin, dec, intra, seg, s0, tm):
    b, r, _ = z.shape
    nb = r // tm
    lc = dec.shape[3]
    fwd = lambda off: pl.BlockSpec((None, tm, V7X_LANES), lambda i, p, k: (i, k, off + p))
    bwd = lambda off: pl.BlockSpec((None, tm, V7X_LANES), lambda i, p, k: (i, nb - 1 - k, off + p))
    tabf = pl.BlockSpec((tm, V7X_LANES), lambda i, p, k: (k, 0))
    tabb = pl.BlockSpec((tm, V7X_LANES), lambda i, p, k: (nb - 1 - k, 0))
    st = pl.BlockSpec((None, None, 2, V7X_LANES, V7X_LANES), lambda i, p, k: (i, p, 0, 0, 0))
    ysd = jax.ShapeDtypeStruct((b, r, 2 * V7X_LANES), F32)
    return pl.pallas_call(
        functools.partial(_ret_body, n_chunks=tm // lc, lc=lc),
        grid=(b, 2, nb),
        in_specs=[fwd(COL_RQ), fwd(COL_RK), fwd(COL_RV), fwd(COL_RGF), tabf, tabf,
                  bwd(COL_RQ), bwd(COL_RK), bwd(COL_RV), bwd(COL_RGB), tabb, tabb,
                  pl.BlockSpec((None, 2, 3, lc, V7X_LANES), lambda i, p, k: (p, 0, 0, 0, 0)),
                  pl.BlockSpec((None, 2, 2, lc, lc), lambda i, p, k: (p, 0, 0, 0, 0)),
                  pl.BlockSpec((V7X_LANES, V7X_LANES), lambda i, p, k: (0, 0)),
                  st],
        out_specs=[pl.BlockSpec((None, tm, V7X_LANES), lambda i, p, k: (i, k, p)),
                   pl.BlockSpec((None, tm, V7X_LANES), lambda i, p, k: (i, nb - 1 - k, p)),
                   st],
        out_shape=[ysd, ysd, jax.ShapeDtypeStruct(s0.shape, F32)],
        scratch_shapes=[pltpu.VMEM((2, V7X_LANES, V7X_LANES), F32)],
        compiler_params=_cparams(("parallel", "parallel", "arbitrary"), 32),
        name="retention",
    )(z, z, z, z, cos, sin, z, z, z, z, cos, sin, dec, intra, seg, s0)


def _sg_body(su_ref, sv_ref, ng_ref, w_ref, bm_ref, o_ref):
    tm = su_ref.shape[0]
    u = jax.nn.gelu(su_ref[...])
    t = jax.nn.gelu(sv_ref[...])
    mu = jnp.mean(t, axis=-1, keepdims=True)
    cen = t - mu
    var = jnp.mean(cen * cen, axis=-1, keepdims=True)
    v = (cen * lax.rsqrt(var + NORM_EPS) * ng_ref[...]).astype(BF16)
    group = lax.broadcasted_iota(jnp.int32, (1, 2 * V7X_LANES), 1) // SG_GROUP_DIM
    bias = bm_ref[...]
    for c in range(tm // SG_CHUNK):
        vc = v[c * SG_CHUNK:(c + 1) * SG_CHUNK, :]
        s = bias
        for g in range(SG_GROUPS):
            s = s + jnp.where(group == g, _dot(w_ref[g], vc), 0.0)
        o_ref[c * SG_CHUNK:(c + 1) * SG_CHUNK, :] = (u[c * SG_CHUNK:(c + 1) * SG_CHUNK, :] * s).astype(BF16)


def spatial_gating(z, ng, w, bm, tm):
    b, r, _ = z.shape
    wide = 2 * V7X_LANES
    return pl.pallas_call(
        _sg_body,
        grid=(b, r // tm),
        in_specs=[pl.BlockSpec((None, tm, wide), lambda i, j: (i, j, COL_SU256)),
                  pl.BlockSpec((None, tm, wide), lambda i, j: (i, j, COL_SV256)),
                  pl.BlockSpec((1, wide), lambda i, j: (0, 0)),
                  pl.BlockSpec((SG_GROUPS, SG_CHUNK, SG_CHUNK), lambda i, j: (0, 0, 0)),
                  pl.BlockSpec((SG_CHUNK, wide), lambda i, j: (0, 0))],
        out_specs=pl.BlockSpec((None, tm, wide), lambda i, j: (i, j, 0)),
        out_shape=jax.ShapeDtypeStruct((b, r, wide), BF16),
        compiler_params=_cparams(("parallel", "parallel"), 32),
        name="spatial_gating",
    )(z, z, ng, w, bm)


def _out_proj_body(a_ref, yf_ref, yb_ref, sg_ref, x_ref, g1_ref, a2_ref, s2_ref, w_ref, rwt_ref,
                   xo_ref, f_ref, aff_ref):
    na = a_ref.shape[1]
    nr = yf_ref.shape[1]
    tm = x_ref.shape[0]
    part = min(tm, 256)
    parts = [slice(s, s + part) for s in range(0, tm, part)]
    ys = []
    for rows in parts:
        r = (yf_ref[rows, :] + yb_ref[rows, :]).astype(BF16)
        ys.append(_dot(a_ref[rows, :], w_ref[0:na, :]) + _dot(r, w_ref[na:na + nr, :])
                  + _dot(sg_ref[rows, :], w_ref[na + nr:, :]))
    for rows, y in zip(parts, ys):
        xn = x_ref[rows, :] + g1_ref[...] * y
        xo_ref[rows, :] = xn
        ms = jnp.mean(xn * xn, axis=-1, keepdims=True)
        f = (xn * lax.rsqrt(ms + NORM_EPS) * a2_ref[...] + s2_ref[...]).astype(BF16)
        f_ref[rows, :] = f
        logits = _dot_nt(rwt_ref[...], f)
        e = jnp.exp(logits - jnp.max(logits, axis=0, keepdims=True))
        aff_ref[:, rows] = e / jnp.sum(e, axis=0, keepdims=True)


def out_proj(a, yf, yb, sg, x, g1, a2, s2, w, rwt, tm):
    b, r, d = x.shape
    rows = lambda width: pl.BlockSpec((None, tm, width), lambda i, j: (i, j, 0))
    vec = pl.BlockSpec((None, 1, d), lambda i, j: (i, 0, 0))
    return pl.pallas_call(
        _out_proj_body,
        grid=(b, r // tm),
        in_specs=[rows(a.shape[2]), rows(yf.shape[2]), rows(yb.shape[2]), rows(sg.shape[2]), rows(d),
                  vec, vec, vec,
                  pl.BlockSpec(w.shape, lambda i, j: (0, 0)),
                  pl.BlockSpec(rwt.shape, lambda i, j: (0, 0))],
        out_specs=[rows(d), rows(d), pl.BlockSpec((None, N_EXPERTS, tm), lambda i, j: (i, 0, j))],
        out_shape=[jax.ShapeDtypeStruct((b, r, d), F32), jax.ShapeDtypeStruct((b, r, d), BF16),
                   jax.ShapeDtypeStruct((b, N_EXPERTS, r), F32)],
        compiler_params=_cparams(("parallel", "parallel"), 48),
        name="out_proj",
    )(a, yf, yb, sg, x, g1, a2, s2, w, rwt)


def _router_body(aff_ref, tri_ref, posm_ref, off_ref, *, cap, tb):
    aff = aff_ref[...]
    e, r = aff.shape
    nb = r // tb
    bits = pltpu.bitcast(aff, jnp.int32)

    def search(i, thr):
        cand = thr | lax.shift_left(jnp.int32(1), 30 - i)
        cnt = jnp.sum(jnp.where(bits >= cand, 1.0, 0.0), axis=1, keepdims=True)
        return jnp.where(cnt >= cap, cand, thr)

    thr = lax.fori_loop(0, 31, search, jnp.zeros((e, 1), jnp.int32))
    gt = jnp.where(bits > thr, 1.0, 0.0)
    eq = jnp.where(bits == thr, 1.0, 0.0)
    need = cap - jnp.sum(gt, axis=1, keepdims=True)
    tri = tri_ref[...]
    lane = lax.broadcasted_iota(jnp.int32, (e, V7X_LANES), 1)
    offs = jnp.zeros((e, V7X_LANES), jnp.int32)
    run_eq = jnp.zeros((e, 1), F32)
    run_sel = jnp.zeros((e, 1), F32)
    for j in range(nb):
        cols = slice(j * tb, (j + 1) * tb)
        eq_j = eq[:, cols]
        before_eq = _dot(eq_j.astype(BF16), tri) + run_eq
        sel_j = gt[:, cols] + eq_j * jnp.where(before_eq < need, 1.0, 0.0)
        pos_j = _dot(sel_j.astype(BF16), tri) + run_sel
        posm_ref[:, cols] = jnp.where(sel_j > 0.0, pos_j, -1.0)
        offs = jnp.where(lane == j, run_sel.astype(jnp.int32), offs)
        run_eq = run_eq + jnp.sum(eq_j, axis=1, keepdims=True)
        run_sel = run_sel + jnp.sum(sel_j, axis=1, keepdims=True)
    off_ref[...] = jnp.where(lane == nb, run_sel.astype(jnp.int32), offs)


def route(aff_t, tri, cap):
    b, e, r = aff_t.shape
    tb = tri.shape[0]
    posm, off = pl.pallas_call(
        functools.partial(_router_body, cap=cap, tb=tb),
        grid=(b,),
        in_specs=[pl.BlockSpec((None, e, r), lambda i: (i, 0, 0)),
                  pl.BlockSpec((tb, tb), lambda i: (0, 0))],
        out_specs=[pl.BlockSpec((None, e, r), lambda i: (i, 0, 0)),
                   pl.BlockSpec((None, e, V7X_LANES), lambda i: (i, 0, 0))],
        out_shape=[jax.ShapeDtypeStruct((b, e, r), F32), jax.ShapeDtypeStruct((b, e, V7X_LANES), jnp.int32)],
        compiler_params=_cparams(("parallel",), 32),
        name="router",
    )(aff_t, tri)
    shape5 = (b, e, r // tb, 1, tb)
    return posm.reshape(shape5), aff_t.reshape(shape5), off.reshape(-1)


def _gather_body(off_ref, f_ref, posm_ref, aff_ref, x_hbm, g_hbm, xall, gall, pcat, sem, *, cap, w, nb, n_exp):
    b = pl.program_id(0)
    j = pl.program_id(1)
    align = V7X_BF16_SUBLANES

    @pl.when(j == 0)
    def _():
        xall[...] = jnp.zeros(xall.shape, BF16)
        gall[...] = jnp.zeros(gall.shape, F32)

    iota = lax.broadcasted_iota(jnp.int32, (w, 1), 0).astype(F32)

    def window(e, rnd):
        off = off_ref[(b * n_exp + e) * V7X_LANES + j]
        return pl.multiple_of(jnp.minimum((off // align) * align + rnd * w, cap), align)

    def one_round(rnd):
        for e in range(n_exp):
            first = window(e, rnd)
            onehot = posm_ref[e] == first.astype(F32) + iota
            pcat[e * w:(e + 1) * w, :] = jnp.where(onehot, 1.0, 0.0).astype(BF16)
            gall[e, pl.ds(first, w), :] += jnp.sum(jnp.where(onehot, aff_ref[e], 0.0), axis=1, keepdims=True)
        rows = _dot(pcat[...], f_ref[...])
        for e in range(n_exp):
            xall[e, pl.ds(window(e, rnd), w), :] += rows[e * w:(e + 1) * w, :].astype(BF16)

    one_round(0)
    span = jnp.int32(0)
    for e in range(n_exp):
        idx = (b * n_exp + e) * V7X_LANES + j
        span = jnp.maximum(span, off_ref[idx + 1] - (off_ref[idx] // align) * align)

    def extra(rnd, carry):
        one_round(rnd)
        return carry

    lax.fori_loop(1, (span + w - 1) // w, extra, 0)

    @pl.when(j == nb - 1)
    def _():
        cx = pltpu.make_async_copy(xall.at[:, pl.ds(0, cap), :], x_hbm.at[b], sem.at[0])
        cg = pltpu.make_async_copy(gall.at[:, pl.ds(0, cap), :], g_hbm.at[b], sem.at[1])
        cx.start()
        cg.start()
        cx.wait()
        cg.wait()


def moe_gather(off, f, posm, aff, cap, w):
    b, r, d = f.shape
    _, e, nb, _, tb = posm.shape
    blk = pl.BlockSpec((None, e, None, 1, tb), lambda i, j, o: (i, 0, j, 0, 0))
    return pl.pallas_call(
        functools.partial(_gather_body, cap=cap, w=w, nb=nb, n_exp=e),
        grid_spec=pltpu.PrefetchScalarGridSpec(
            num_scalar_prefetch=1,
            grid=(b, nb),
            in_specs=[pl.BlockSpec((None, tb, d), lambda i, j, o: (i, j, 0)), blk, blk],
            out_specs=[pl.BlockSpec(memory_space=pl.ANY), pl.BlockSpec(memory_space=pl.ANY)],
            scratch_shapes=[pltpu.VMEM((e, cap + w, d), BF16), pltpu.VMEM((e, cap + w, 1), F32),
                            pltpu.VMEM((e * w, tb), BF16), pltpu.SemaphoreType.DMA((2,))]),
        out_shape=[jax.ShapeDtypeStruct((b, e, cap, d), BF16), jax.ShapeDtypeStruct((b, e, cap, 1), F32)],
        compiler_params=_cparams(("arbitrary", "arbitrary"), 56),
        name="moe_gather",
    )(off, f, posm, aff)


def _moe_ffn_body(*refs, has_ctx, cap, capc):
    if has_ctx:
        _, x_ref, g_ref, xc_ref, gc_ref, wg_ref, wu_ref, wd_ref, y_ref, yc_ref, yacc, ycacc = refs
    else:
        _, x_ref, g_ref, wg_ref, wu_ref, wd_ref, y_ref, yacc = refs
    c = pl.program_id(1)
    last = pl.num_programs(1) - 1
    nbatch, _, d = x_ref.shape
    wg = wg_ref[...].astype(BF16)
    wu = wu_ref[...].astype(BF16)
    wd = wd_ref[...].astype(BF16)

    def hidden(x):
        a = _dot(x, wg)
        u = _dot(x, wu)
        return (a * _sigmoid(a) * u).astype(BF16)

    @pl.when(c == 0)
    def _():
        yacc[...] = jnp.zeros(yacc.shape, F32)
        if has_ctx:
            ycacc[...] = jnp.zeros(ycacc.shape, F32)

    xs = [x_ref[i] for i in range(nbatch)]
    if has_ctx:
        xs[-1] = jnp.concatenate([xs[-1], xc_ref[...].reshape(nbatch * capc, d)], axis=0)
    hs = [hidden(x) for x in xs]
    ys = [_dot(h, wd) for h in hs]
    for i in range(nbatch):
        yacc[i] += ys[i][0:cap]
    if has_ctx:
        ycacc[...] += ys[-1][cap:]

    @pl.when(c == last)
    def _():
        for i in range(nbatch):
            y_ref[i, 0:cap, :] = (yacc[i] * g_ref[i]).astype(BF16)
            y_ref[i, cap:, :] = jnp.zeros((y_ref.shape[1] - cap, d), BF16)
        if has_ctx:
            yc = ycacc[...] * gc_ref[...].reshape(nbatch * capc, 1)
            yc_ref[:, 0:capc, :] = yc.reshape(nbatch, capc, d).astype(BF16)
            yc_ref[:, capc:, :] = jnp.zeros((nbatch, yc_ref.shape[1] - capc, d), BF16)


def moe_ffn(layer, x, g, xc, gc, wg, wu, wd, tf, pad, padc):
    b, e, cap, d = x.shape
    ff = wg.shape[3]
    has_ctx = xc is not None
    capc = xc.shape[2] if has_ctx else 0
    tok = lambda rows, width: pl.BlockSpec((b, None, rows, width), lambda j, k, l: (0, j, 0, 0))
    in_specs = [tok(cap, d), tok(cap, 1)]
    out_specs = [tok(cap + pad, d)]
    out_shape = [jax.ShapeDtypeStruct((b, e, cap + pad, d), BF16)]
    scratch = [pltpu.VMEM((b, cap, d), F32)]
    args = [x, g]
    if has_ctx:
        in_specs += [tok(capc, d), tok(capc, 1)]
        out_specs.append(tok(capc + padc, d))
        out_shape.append(jax.ShapeDtypeStruct((b, e, capc + padc, d), BF16))
        scratch.append(pltpu.VMEM((b * capc, d), F32))
        args += [xc, gc]
    in_specs += [pl.BlockSpec((None, None, d, tf), lambda j, k, l: (l[0], j, 0, k)),
                 pl.BlockSpec((None, None, d, tf), lambda j, k, l: (l[0], j, 0, k)),
                 pl.BlockSpec((None, None, tf, d), lambda j, k, l: (l[0], j, k, 0))]
    out = pl.pallas_call(
        functools.partial(_moe_ffn_body, has_ctx=has_ctx, cap=cap, capc=capc),
        grid_spec=pltpu.PrefetchScalarGridSpec(
            num_scalar_prefetch=1, grid=(e, ff // tf),
            in_specs=in_specs, out_specs=out_specs, scratch_shapes=scratch),
        out_shape=out_shape,
        compiler_params=_cparams(("parallel", "arbitrary"), 56),
        name="moe_ffn",
    )(layer, *args, wg, wu, wd)
    return out if has_ctx else (out[0], None)


def _scatter_body(off_ref, x_ref, g2_ref, posm_ref, y_hbm, xo_ref, ybuf, pcat, sem, *, cap, w, nb, n_exp):
    b = pl.program_id(0)
    j = pl.program_id(1)
    step = b * nb + j
    slot = lax.rem(step, 2)
    spare = 2
    align = V7X_BF16_SUBLANES

    def window(bb, jj, e, rnd):
        off = off_ref[(bb * n_exp + e) * V7X_LANES + jj]
        return jnp.minimum((off // align) * align + rnd * w, cap)

    def copies(bb, jj, buf, rnd):
        return [pltpu.make_async_copy(
            y_hbm.at[bb, e, pl.ds(pl.multiple_of(window(bb, jj, e, rnd), align), w), :],
            ybuf.at[buf, pl.ds(e * w, w), :], sem.at[buf]) for e in range(n_exp)]

    @pl.when(step == 0)
    def _():
        for cp in copies(b, j, 0, 0):
            cp.start()

    @pl.when(step + 1 < pl.num_programs(0) * nb)
    def _():
        nxt = step + 1
        for cp in copies(nxt // nb, lax.rem(nxt, nb), 1 - slot, 0):
            cp.start()

    iota = lax.broadcasted_iota(jnp.int32, (w, 1), 0).astype(F32)

    def onehots(rnd):
        for e in range(n_exp):
            first = window(b, j, e, rnd).astype(F32)
            pcat[e * w:(e + 1) * w, :] = jnp.where(posm_ref[e] == first + iota, 1.0, 0.0).astype(BF16)

    for cp in copies(b, j, slot, 0):
        cp.wait()
    onehots(0)
    g2 = g2_ref[...]
    xo_ref[...] = x_ref[...] + g2 * _dot_tn(pcat[...], ybuf[slot])

    span = jnp.int32(0)
    for e in range(n_exp):
        idx = (b * n_exp + e) * V7X_LANES + j
        span = jnp.maximum(span, off_ref[idx + 1] - (off_ref[idx] // align) * align)

    def extra(rnd, carry):
        for cp in copies(b, j, spare, rnd):
            cp.start()
        for cp in copies(b, j, spare, rnd):
            cp.wait()
        onehots(rnd)
        xo_ref[...] += g2 * _dot_tn(pcat[...], ybuf[spare])
        return carry

    lax.fori_loop(1, (span + w - 1) // w, extra, 0)


def moe_scatter(off, x, g2, posm, y, cap, w):
    b, r, d = x.shape
    _, e, nb, _, tb = posm.shape
    rows = pl.BlockSpec((None, tb, d), lambda i, j, o: (i, j, 0))
    return pl.pallas_call(
        functools.partial(_scatter_body, cap=cap, w=w, nb=nb, n_exp=e),
        grid_spec=pltpu.PrefetchScalarGridSpec(
            num_scalar_prefetch=1,
            grid=(b, nb),
            in_specs=[rows, pl.BlockSpec((None, 1, d), lambda i, j, o: (i, 0, 0)),
                      pl.BlockSpec((None, e, None, 1, tb), lambda i, j, o: (i, 0, j, 0, 0)),
                      pl.BlockSpec(memory_space=pl.ANY)],
            out_specs=rows,
            scratch_shapes=[pltpu.VMEM((3, e * w, d), BF16), pltpu.VMEM((e * w, tb), BF16),
                            pltpu.SemaphoreType.DMA((3,))]),
        out_shape=jax.ShapeDtypeStruct((b, r, d), F32),
        compiler_params=_cparams(("arbitrary", "arbitrary"), 32),
        name="moe_scatter",
    )(off, x, g2, posm, y)


def _attn_tables(n):
    rows = n // GRID_W
    row = jnp.broadcast_to(jnp.arange(rows, dtype=F32)[:, None], (rows, GRID_W)).reshape(-1)
    col = jnp.broadcast_to(jnp.arange(GRID_W, dtype=F32)[None, :], (rows, GRID_W)).reshape(-1)
    quarter = DA_HEAD_DIM // 4
    inv = ROPE_BASE ** (-jnp.arange(quarter, dtype=F32) / quarter)
    ang_r, ang_c = row[:, None] * inv, col[:, None] * inv

    def half(ang):
        return (jnp.concatenate([jnp.cos(ang), jnp.cos(ang)], -1),
                jnp.concatenate([-jnp.sin(ang), jnp.sin(ang)], -1))

    cr, sr = half(ang_r)
    cc, sc = half(ang_c)
    cos64 = jnp.concatenate([cr, cc], -1)
    sin64 = jnp.concatenate([sr, sc], -1)
    return jnp.tile(cos64, (1, 2)), jnp.tile(sin64, (1, 2))


def _ret_tables(n):
    inv = 1.0 / (ROPE_BASE ** jnp.linspace(0.0, 1.0, RET_QK_DIM // 2, dtype=F32))
    ang = jnp.arange(n, dtype=F32)[:, None] * inv
    cos64 = jnp.concatenate([jnp.cos(ang), jnp.cos(ang)], -1)
    sin64 = jnp.concatenate([-jnp.sin(ang), jnp.sin(ang)], -1)
    return jnp.tile(cos64, (1, 2)), jnp.tile(sin64, (1, 2))


def _ret_decay_tables(log_gamma, lc):
    pos = jnp.arange(lc, dtype=F32)
    lg = jnp.repeat(log_gamma, RET_QK_DIM, axis=1).reshape(2, 2, 1, V7X_LANES)
    qd = jnp.stack([jnp.exp(lg[0] * (pos + 1.0)[None, :, None]), jnp.exp(lg[1] * (lc - pos)[None, :, None])])
    kd = jnp.stack([jnp.exp(lg[0] * (lc - 1.0 - pos)[None, :, None]), jnp.exp(lg[1] * pos[None, :, None])])
    cd = jnp.broadcast_to(jnp.exp(lg * lc), (2, 2, lc, V7X_LANES))
    dec = jnp.stack([qd, kd, cd], axis=2)
    dist = pos[:, None] - pos[None, :]
    lgh = log_gamma[:, :, None, None]
    fwd = jnp.where(dist >= 0, jnp.exp(lgh[0] * jnp.maximum(dist, 0.0)), 0.0)
    bwd = jnp.where(dist <= 0, jnp.exp(lgh[1] * jnp.maximum(-dist, 0.0)), 0.0)
    intra = jnp.stack([fwd, bwd]).reshape(2, 2, 2, lc, lc)
    return jnp.swapaxes(dec, 0, 1), jnp.swapaxes(intra, 0, 1)


def kernel(x, c, ctx, c_ctx, w_mod, b_mod, norm1_g, norm2_g, w_in, w_out, da_q_norm_g, da_k_norm_g,
           da_lambda_q1, da_lambda_k1, da_lambda_q2, da_lambda_k2, da_subln_g, ret_decay,
           sg_norm_g, sg_w, sg_b, router_w, ex_w_gate, ex_w_up, ex_w_down):
    b, n, d = x.shape
    n_ctx = ctx.shape[1]
    cos_a, sin_a = _attn_tables(n)
    cos_r, sin_r = _ret_tables(n)
    one_c = jnp.ones((n_ctx, V7X_LANES), F32)
    zero_c = jnp.zeros((n_ctx, V7X_LANES), F32)
    lane = jnp.arange(V7X_LANES)
    seg = (lane[:, None] // 64 == lane[None, :] // 64).astype(BF16)
    state0 = jnp.zeros((b, 2, 2, V7X_LANES, V7X_LANES), F32)
    tk = n_ctx
    assert (n + n_ctx) // tk % ATTN_BLOCKS_PER_STEP == 0
    tok = jnp.arange(MOE_TOKEN_BLOCK)
    tri = (tok[:, None] < tok[None, :]).astype(BF16)
    cap = EC_CAPACITY * n // N_EXPERTS
    cap_c = EC_CAPACITY * n_ctx // N_EXPERTS
    pad, pad_c = min(64, cap), min(64, cap_c)

    for layer in range(DEPTH):
        need_ctx = layer < DEPTH - 1
        lambda_init = 0.8 - 0.6 * math.exp(-0.3 * layer)
        mod = jax.nn.silu(c) @ w_mod[layer] + b_mod[layer]
        mod_c = jnp.broadcast_to(jax.nn.silu(c_ctx) @ w_mod[layer] + b_mod[layer], mod.shape)
        sh1, sc1, g1, sh2, sc2, g2 = jnp.split(mod[:, None, :], 6, axis=-1)
        csh1, csc1, cg1, csh2, csc2, cg2 = jnp.split(mod_c[:, None, :], 6, axis=-1)
        lam = (jnp.exp(jnp.sum(da_lambda_q1[layer] * da_lambda_k1[layer]))
               - jnp.exp(jnp.sum(da_lambda_q2[layer] * da_lambda_k2[layer])) + lambda_init).reshape(1)
        log_gamma = jax.nn.log_sigmoid(ret_decay[layer].astype(F32))
        dec, intra = _ret_decay_tables(log_gamma, RET_SCAN_CHUNK)
        w_in_b = w_in[layer].astype(BF16)
        w_out_b = w_out[layer].astype(BF16)
        rwt = router_w[layer].T.astype(BF16)
        qg = jnp.tile(da_q_norm_g[layer], 2)[None, :]
        kg = jnp.tile(da_k_norm_g[layer], 2)[None, :]
        gcol = (da_subln_g[layer] * (1.0 - lambda_init))[:, None]
        sg_ng = sg_norm_g[layer][None, :]
        sg_wb = sg_w[layer].astype(BF16)
        sg_bm = jnp.repeat(sg_b[layer].T, SG_GROUP_DIM, axis=1)
        n1, n2 = norm1_g[layer][None, None, :], norm2_g[layer][None, None, :]

        z_lat = in_proj(x, n1 * (1.0 + sc1), sh1, w_in_b, 512)
        z_ctx = in_proj(ctx, n1 * (1.0 + csc1), csh1, w_in_b, n_ctx)

        cq1, cq2, ck, cvt = qkv_prep(z_ctx, one_c, zero_c, qg, kg, seg, n_ctx, tk)
        q1, q2, kl, vlt = qkv_prep(z_lat, cos_a, sin_a, qg, kg, seg, _row_tile(n, 1024), tk)
        sources = [(ck, cvt), (kl, vlt)]
        bound = score_bound(da_q_norm_g[layer], da_k_norm_g[layer])
        c1, c2 = softmax_shifts(bound, q1, q2, [ck, kl], ATTN_QUERY_TILE, tk)
        a_lat = diff_attention(lam, q1, q2, c1, c2, sources, gcol, ATTN_QUERY_TILE, ATTN_BLOCKS_PER_STEP)

        cyf, cyb, cstate = retention(z_ctx, one_c, zero_c, dec, intra, seg, state0, n_ctx)
        yf, yb, _ = retention(z_lat, cos_r, sin_r, dec, intra, seg, cstate, _row_tile(n, 1024))

        s_lat = spatial_gating(z_lat, sg_ng, sg_wb, sg_bm, 512)

        x, f_lat, aff_lat = out_proj(a_lat, yf, yb, s_lat, x, g1, n2 * (1.0 + sc2), sh2, w_out_b, rwt,
                                     _row_tile(n, 1024))
        posm, aff5, off = route(aff_lat, tri, cap)
        xs, gs = moe_gather(off, f_lat, posm, aff5, cap, pad)
        xcs = gcs = None
        if need_ctx:
            cc1, cc2 = softmax_shifts(bound, cq1, cq2, [ck], n_ctx, tk)
            a_ctx = diff_attention(lam, cq1, cq2, cc1, cc2, sources[:1], gcol, n_ctx, 1)
            s_ctx = spatial_gating(z_ctx, sg_ng, sg_wb, sg_bm, n_ctx)
            ctx, f_ctx, aff_ctx = out_proj(a_ctx, cyf, cyb, s_ctx, ctx, cg1, n2 * (1.0 + csc2), csh2,
                                           w_out_b, rwt, n_ctx)
            posm_c, aff5_c, off_c = route(aff_ctx, tri, cap_c)
            xcs, gcs = moe_gather(off_c, f_ctx, posm_c, aff5_c, cap_c, pad_c)
        ys, ycs = moe_ffn(jnp.full((1,), layer, jnp.int32), xs, gs, xcs, gcs, ex_w_gate, ex_w_up, ex_w_down,
                          512, pad, pad_c)
        x = moe_scatter(off, x, g2, posm, ys, cap, pad)
        if need_ctx:
            ctx = moe_scatter(off_c, ctx, cg2, posm_c, ycs, cap_c, pad_c)
    return x
```

```python
import functools
import math

import jax
import jax.numpy as jnp
from jax import lax
from jax.experimental import pallas as pl
from jax.experimental.pallas import tpu as pltpu

F32 = jnp.float32
BF16 = jnp.bfloat16

D_MODEL = 1024
DEPTH = 4
GRID_W = 64
NORM_EPS = 1e-6
ROPE_BASE = 10000.0
DA_HEADS = 4
DA_HEAD_DIM = 64
RET_HEADS = 4
RET_QK_DIM = 64
RET_SCAN_CHUNK = 128
SG_GROUPS = 4
SG_GROUP_DIM = 64
SG_CHUNK = 128
N_EXPERTS = 16
EXPERT_FF = 2 * D_MODEL
EC_CAPACITY = 2
IN_COLS = 3328
COL_AQ, COL_AK, COL_AV = 0, 4, 8
COL_RQ, COL_RK, COL_RV, COL_RGF, COL_RGB = 12, 14, 16, 18, 20
COL_SU256, COL_SV256 = 11, 12

V7X_LANES = 128
V7X_BF16_SUBLANES = 16
V7X_VMEM_BYTES = 64 * 1024 * 1024
MOE_TOKEN_BLOCK = 256
ATTN_BLOCKS_PER_STEP = 11
ATTN_QUERY_TILE = 512
NEG_BIG = -1e30
F32_EXP_RANGE = 83.0


def _cparams(semantics, vmem_mib):
    assert vmem_mib * 1024 * 1024 < V7X_VMEM_BYTES
    return pltpu.CompilerParams(dimension_semantics=semantics, vmem_limit_bytes=vmem_mib * 1024 * 1024)


def _row_tile(rows, preferred):
    while rows % preferred:
        preferred //= 2
    return preferred


def _dot(a, b):
    return jnp.dot(a, b, preferred_element_type=F32)


def _dot_nt(a, b):
    return lax.dot_general(a, b, (((1,), (1,)), ((), ())), preferred_element_type=F32)


def _dot_tn(a, b):
    return lax.dot_general(a, b, (((0,), (0,)), ((), ())), preferred_element_type=F32)


def _seg_dot(x, seg):
    hi = x.astype(BF16)
    lo = (x - hi.astype(F32)).astype(BF16)
    return _dot(hi, seg) + _dot(lo, seg)


def _sigmoid(x):
    return 1.0 / (1.0 + jnp.exp(-x))


def _modulation_body(c_ref, w_ref, b_ref, o_ref):
    cv = c_ref[...]
    o_ref[...] = _dot((cv * _sigmoid(cv)).astype(BF16), w_ref[...].astype(BF16)) + b_ref[...]


def modulation(cvecs, w_mod, b_mod):
    depth, d, cols = w_mod.shape
    rows = -(-cvecs.shape[0] // 8) * 8
    cpad = jnp.concatenate([cvecs, jnp.zeros((rows - cvecs.shape[0], d), F32)], axis=0)
    tn = cols // 4
    return pl.pallas_call(
        _modulation_body,
        grid=(depth, cols // tn),
        in_specs=[pl.BlockSpec((rows, d), lambda l, j: (0, 0)),
                  pl.BlockSpec((None, d, tn), lambda l, j: (l, 0, j)),
                  pl.BlockSpec((None, 1, tn), lambda l, j: (l, 0, j))],
        out_specs=pl.BlockSpec((None, rows, tn), lambda l, j: (l, 0, j)),
        out_shape=jax.ShapeDtypeStruct((depth, rows, cols), F32),
        compiler_params=_cparams(("parallel", "parallel"), 32),
        name="modulation",
    )(cpad, w_mod, b_mod[:, None, :])


def _in_proj_body(x_ref, a_ref, s_ref, w_ref, z_ref):
    x = x_ref[...]
    ms = jnp.mean(x * x, axis=-1, keepdims=True)
    h = x * lax.rsqrt(ms + NORM_EPS) * a_ref[...] + s_ref[...]
    z_ref[...] = _dot(h.astype(BF16), w_ref[...])


def in_proj(x, a, s, w, tm):
    b, r, d = x.shape
    c = w.shape[1]
    return pl.pallas_call(
        _in_proj_body,
        grid=(b, r // tm),
        in_specs=[
            pl.BlockSpec((None, tm, d), lambda i, j: (i, j, 0)),
            pl.BlockSpec((None, 1, d), lambda i, j: (i, 0, 0)),
            pl.BlockSpec((None, 1, d), lambda i, j: (i, 0, 0)),
            pl.BlockSpec((d, c), lambda i, j: (0, 0)),
        ],
        out_specs=pl.BlockSpec((None, tm, c), lambda i, j: (i, j, 0)),
        out_shape=jax.ShapeDtypeStruct((b, r, c), F32),
        compiler_params=_cparams(("parallel", "parallel"), 48),
        name="in_proj",
    )(x, a, s, w)


def _qkv_prep_body(q_ref, k_ref, v_ref, cos_ref, sin_ref, qg_ref, kg_ref, seg_ref,
                   qp1_ref, qp2_ref, kk_ref, vt_ref):
    lane = lax.broadcasted_iota(jnp.int32, (1, V7X_LANES), 1)
    lower16 = (lane % 32) < 16
    seg = seg_ref[...]
    cos = cos_ref[...]
    sin = sin_ref[...]

    def norm_rope(x, g):
        ms = _seg_dot(x * x, seg) * (1.0 / DA_HEAD_DIM)
        y = x * lax.rsqrt(ms + NORM_EPS) * g
        partner = jnp.where(lower16, pltpu.roll(y, 112, 1), pltpu.roll(y, 16, 1))
        return y * cos + partner * sin

    q = norm_rope(q_ref[...], qg_ref[...]) * DA_HEAD_DIM ** -0.5
    first = lane < DA_HEAD_DIM
    qp1_ref[...] = jnp.where(first, q, 0.0).astype(BF16)
    qp2_ref[...] = jnp.where(first, 0.0, q).astype(BF16)
    kk_ref[...] = norm_rope(k_ref[...], kg_ref[...]).astype(BF16)
    tv = vt_ref.shape[2]
    for s in range(vt_ref.shape[0]):
        vt_ref[s] = v_ref[s * tv:(s + 1) * tv, :].T.astype(BF16)


def qkv_prep(z, cos, sin, qg, kg, seg, tm, tv):
    b, r, _ = z.shape
    h = DA_HEADS
    row = lambda off: pl.BlockSpec((None, tm, V7X_LANES), lambda i, j, k: (i, k, off + j))
    tab = pl.BlockSpec((tm, V7X_LANES), lambda i, j, k: (k, 0))
    vec = pl.BlockSpec((1, V7X_LANES), lambda i, j, k: (0, 0))
    o4 = pl.BlockSpec((None, None, tm, V7X_LANES), lambda i, j, k: (i, j, k, 0))
    sd = jax.ShapeDtypeStruct((b, h, r, V7X_LANES), BF16)
    return pl.pallas_call(
        _qkv_prep_body,
        grid=(b, h, r // tm),
        in_specs=[row(COL_AQ), row(COL_AK), row(COL_AV), tab, tab, vec, vec,
                  pl.BlockSpec((V7X_LANES, V7X_LANES), lambda i, j, k: (0, 0))],
        out_specs=[o4, o4, o4,
                   pl.BlockSpec((None, None, tm // tv, V7X_LANES, tv), lambda i, j, k: (i, j, k, 0, 0))],
        out_shape=[sd, sd, sd, jax.ShapeDtypeStruct((b, h, r // tv, V7X_LANES, tv), BF16)],
        compiler_params=_cparams(("parallel", "parallel", "parallel"), 32),
        name="qkv_prep",
    )(z, z, z, cos, sin, qg, kg, seg)


def _attn_body(*refs, bps, n_src):
    lam_ref, q1_ref, q2_ref, c1_ref, c2_ref = refs[:5]
    srcs = [(refs[5 + 2 * s], refs[6 + 2 * s]) for s in range(n_src)]
    g_ref, o_ref, acc_ref = refs[5 + 2 * n_src:]
    q1 = q1_ref[...]
    q2 = q2_ref[...]
    c1 = c1_ref[...]
    c2 = c2_ref[...]
    tq = q1.shape[0]
    tk = srcs[0][1].shape[2]
    blocks = [(s, j) for s, (_, v_ref) in enumerate(srcs) for j in range(v_ref.shape[0])]
    n_steps = len(blocks) // bps

    def runs(i):
        out = []
        for s, j in blocks[i * bps:(i + 1) * bps]:
            if out and out[-1][0] == s and out[-1][2] == j:
                out[-1][2] = j + 1
            else:
                out.append([s, j, j + 1])
        return out

    def probs(i):
        segs, d1, d2 = [], 0.0, 0.0
        for s, lo, hi in runs(i):
            k = srcs[s][0][lo * tk:hi * tk, :]
            p1 = jnp.exp(_dot_nt(k, q1) - c1)
            p2 = jnp.exp(_dot_nt(k, q2) - c2)
            segs.append((s, lo, hi, (p1.astype(BF16), p2.astype(BF16))))
            d1 = d1 + jnp.sum(p1.reshape(-1, 8, tq), axis=0)
            d2 = d2 + jnp.sum(p2.reshape(-1, 8, tq), axis=0)
        return segs, d1, d2

    def accumulate(segs):
        for idx in range(2):
            part = None
            for s, lo, hi, ps in segs:
                for j in range(lo, hi):
                    term = _dot(srcs[s][1][j], ps[idx][(j - lo) * tk:(j - lo + 1) * tk])
                    part = term if part is None else part + term
            acc_ref[idx] += part

    acc_ref[...] = jnp.zeros(acc_ref.shape, F32)
    segs, l1, l2 = probs(0)
    for i in range(n_steps - 1):
        nxt, d1, d2 = probs(i + 1)
        accumulate(segs)
        segs, l1, l2 = nxt, l1 + d1, l2 + d2
    accumulate(segs)
    l1 = jnp.sum(l1, axis=0, keepdims=True)
    l2 = jnp.sum(l2, axis=0, keepdims=True)
    o = acc_ref[0] / l1 - lam_ref[0] * (acc_ref[1] / l2)
    ms = jnp.mean(o * o, axis=0, keepdims=True)
    y = o * lax.rsqrt(ms + NORM_EPS) * g_ref[...]
    o_ref[...] = y.T.astype(BF16)


def diff_attention(lam, q1, q2, c1, c2, sources, gcol, tq, bps):
    b, h, nq, _ = q1.shape
    qs = pl.BlockSpec((None, None, tq, V7X_LANES), lambda i, j, s: (i, j, s, 0))
    cs = pl.BlockSpec((None, None, 1, tq), lambda i, j, s: (i, j, 0, s))
    in_specs = [pl.BlockSpec(memory_space=pltpu.SMEM), qs, qs, cs, cs]
    args = [lam, q1, q2, c1, c2]
    for k, vt in sources:
        in_specs += [pl.BlockSpec((None, None) + k.shape[2:], lambda i, j, s: (i, j, 0, 0)),
                     pl.BlockSpec((None, None) + vt.shape[2:], lambda i, j, s: (i, j, 0, 0, 0))]
        args += [k, vt]
    assert sum(vt.shape[2] for _, vt in sources) % bps == 0
    return pl.pallas_call(
        functools.partial(_attn_body, bps=bps, n_src=len(sources)),
        grid=(b, h, nq // tq),
        in_specs=in_specs + [pl.BlockSpec((V7X_LANES, 1), lambda i, j, s: (0, 0))],
        out_specs=pl.BlockSpec((None, tq, V7X_LANES), lambda i, j, s: (i, s, j)),
        out_shape=jax.ShapeDtypeStruct((b, nq, h * V7X_LANES), BF16),
        scratch_shapes=[pltpu.VMEM((2, V7X_LANES, tq), F32)],
        compiler_params=_cparams(("parallel", "parallel", "parallel"), 48),
        name="diff_attention",
    )(*args, gcol)


def _rowmax_body(q1_ref, q2_ref, k_ref, m1_ref, m2_ref, *, n_blocks, tk):
    q1 = q1_ref[...]
    q2 = q2_ref[...]

    def body(i, carry):
        k = k_ref[pl.ds(pl.multiple_of(i * tk, tk), tk), :]
        return (jnp.maximum(carry[0], jnp.max(_dot_nt(k, q1), axis=0, keepdims=True)),
                jnp.maximum(carry[1], jnp.max(_dot_nt(k, q2), axis=0, keepdims=True)))

    start = jnp.full((1, q1.shape[0]), NEG_BIG, F32)
    m1, m2 = lax.fori_loop(0, n_blocks, body, (start, start))
    m1_ref[...] = m1
    m2_ref[...] = m2


def score_rowmax(q1, q2, k, tq, tk):
    b, h, nq, _ = q1.shape
    nk = k.shape[2]
    qs = pl.BlockSpec((None, None, tq, V7X_LANES), lambda i, j, s: (i, j, s, 0))
    ms = pl.BlockSpec((None, None, 1, tq), lambda i, j, s: (i, j, 0, s))
    sd = jax.ShapeDtypeStruct((b, h, 1, nq), F32)
    return pl.pallas_call(
        functools.partial(_rowmax_body, n_blocks=nk // tk, tk=tk),
        grid=(b, h, nq // tq),
        in_specs=[qs, qs, pl.BlockSpec((None, None, nk, V7X_LANES), lambda i, j, s: (i, j, 0, 0))],
        out_specs=[ms, ms],
        out_shape=[sd, sd],
        compiler_params=_cparams(("parallel", "parallel", "parallel"), 32),
        name="score_rowmax",
    )(q1, q2, k)


def score_bound(qg, kg):
    return 1.01 * DA_HEAD_DIM ** 0.5 * jnp.max(jnp.abs(qg)) * jnp.max(jnp.abs(kg))


def softmax_shifts(bound, q1, q2, keys, tq, tk):
    shape = q1.shape[:2] + (1, q1.shape[2])
    flat = lambda: (jnp.full(shape, bound, F32),) * 2
    exact = lambda: tuple(score_rowmax(q1, q2, jnp.concatenate(keys, axis=2), tq, tk))
    return lax.cond(2.0 * bound <= F32_EXP_RANGE, flat, exact)


def _ret_body(qf_ref, kf_ref, vf_ref, gf_ref, cosf_ref, sinf_ref,
              qb_ref, kb_ref, vb_ref, gb_ref, cosb_ref, sinb_ref,
              dec_ref, intra_ref, seg_ref, s0_ref,
              yf_ref, yb_ref, sout_ref, state_ref, *, n_chunks, lc):
    step = pl.program_id(2)
    @pl.when(step == 0)
    def _():
        state_ref[...] = s0_ref[...]

    lane = lax.broadcasted_iota(jnp.int32, (1, V7X_LANES), 1)
    lower32 = (lane % RET_QK_DIM) < (RET_QK_DIM // 2)
    head0 = lane < RET_QK_DIM
    blk = (lax.broadcasted_iota(jnp.int32, (V7X_LANES, V7X_LANES), 0) // RET_QK_DIM ==
           lax.broadcasted_iota(jnp.int32, (V7X_LANES, V7X_LANES), 1) // RET_QK_DIM)
    seg = seg_ref[...]

    def rot(x, cos, sin):
        partner = jnp.where(lower32, pltpu.roll(x, 96, 1), pltpu.roll(x, 32, 1))
        return x * cos + partner * sin

    fwd_refs = (qf_ref, kf_ref, vf_ref, gf_ref, cosf_ref, sinf_ref, yf_ref)
    bwd_refs = (qb_ref, kb_ref, vb_ref, gb_ref, cosb_ref, sinb_ref, yb_ref)
    work = ([(0, c, fwd_refs) for c in range(n_chunks)] + [(1, c, bwd_refs) for c in reversed(range(n_chunks))])
    heads = (head0, jnp.logical_not(head0))

    def scores(d, c, refs):
        q_ref, k_ref, v_ref, _, cos_ref, sin_ref, _ = refs
        rows = pl.ds(c * lc, lc)
        cos = cos_ref[rows, :]
        sin = sin_ref[rows, :]
        q = rot(q_ref[rows, :], cos, sin)
        k = rot(k_ref[rows, :], cos, sin) * (RET_QK_DIM ** -0.5)
        v = v_ref[rows, :]
        kb = k.astype(BF16)
        sc = [_dot_nt(jnp.where(sel, q, 0.0).astype(BF16), kb) for sel in heads]
        kv = _dot_tn((k * dec_ref[d, 1]).astype(BF16), v.astype(BF16))
        return q.astype(BF16), v, sc, jnp.where(blk, kv, 0.0)

    stage1 = [scores(*w) for w in work]

    states = []
    for d in range(2):
        s = state_ref[d]
        cdec = dec_ref[d, 2][0:1, :]
        for (wd, _, _), (_, _, _, kv) in zip(work, stage1):
            if wd == d:
                states.append(s)
                s = s * cdec + kv
        state_ref[d] = s

    outs = []
    for (d, _, _), (qb, v, sc, _), s in zip(work, stage1, states):
        o = _dot(qb, s.astype(BF16)) * dec_ref[d, 0]
        for hh, sel in enumerate(heads):
            o = o + _dot((sc[hh] * intra_ref[d, hh]).astype(BF16), jnp.where(sel, v, 0.0).astype(BF16))
        outs.append(o)
    cens = [o - _seg_dot(o, seg) * (1.0 / RET_QK_DIM) for o in outs]
    variances = [_seg_dot(cen * cen, seg) * (1.0 / RET_QK_DIM) for cen in cens]
    for (_, c, refs), cen, var in zip(work, cens, variances):
        g = refs[3][pl.ds(c * lc, lc), :]
        refs[6][pl.ds(c * lc, lc), :] = cen * lax.rsqrt(var + NORM_EPS) * (g * _sigmoid(g))

    @pl.when(step == pl.num_programs(2) - 1)
    def _():
        sout_ref[...] = state_ref[...]


def retention(z, cos, sin, dec, intra, seg, s0, tm):
    b, r, _ = z.shape
    nb = r // tm
    lc = dec.shape[3]
    fwd = lambda off: pl.BlockSpec((None, tm, V7X_LANES), lambda i, p, k: (i, k, off + p))
    bwd = lambda off: pl.BlockSpec((None, tm, V7X_LANES), lambda i, p, k: (i, nb - 1 - k, off + p))
    tabf = pl.BlockSpec((tm, V7X_LANES), lambda i, p, k: (k, 0))
    tabb = pl.BlockSpec((tm, V7X_LANES), lambda i, p, k: (nb - 1 - k, 0))
    st = pl.BlockSpec((None, None, 2, V7X_LANES, V7X_LANES), lambda i, p, k: (i, p, 0, 0, 0))
    ysd = jax.ShapeDtypeStruct((b, r, 2 * V7X_LANES), F32)
    return pl.pallas_call(
        functools.partial(_ret_body, n_chunks=tm // lc, lc=lc),
        grid=(b, 2, nb),
        in_specs=[fwd(COL_RQ), fwd(COL_RK), fwd(COL_RV), fwd(COL_RGF), tabf, tabf,
                  bwd(COL_RQ), bwd(COL_RK), bwd(COL_RV), bwd(COL_RGB), tabb, tabb,
                  pl.BlockSpec((None, 2, 3, lc, V7X_LANES), lambda i, p, k: (p, 0, 0, 0, 0)),
                  pl.BlockSpec((None, 2, 2, lc, lc), lambda i, p, k: (p, 0, 0, 0, 0)),
                  pl.BlockSpec((V7X_LANES, V7X_LANES), lambda i, p, k: (0, 0)),
                  st],
        out_specs=[pl.BlockSpec((None, tm, V7X_LANES), lambda i, p, k: (i, k, p)),
                   pl.BlockSpec((None, tm, V7X_LANES), lambda i, p, k: (i, nb - 1 - k, p)),
                   st],
        out_shape=[ysd, ysd, jax.ShapeDtypeStruct(s0.shape, F32)],
        scratch_shapes=[pltpu.VMEM((2, V7X_LANES, V7X_LANES), F32)],
        compiler_params=_cparams(("parallel", "parallel", "arbitrary"), 32),
        name="retention",
    )(z, z, z, z, cos, sin, z, z, z, z, cos, sin, dec, intra, seg, s0)


def _sg_body(su_ref, sv_ref, ng_ref, w_ref, bm_ref, o_ref):
    tm = su_ref.shape[0]
    u = jax.nn.gelu(su_ref[...])
    t = jax.nn.gelu(sv_ref[...])
    mu = jnp.mean(t, axis=-1, keepdims=True)
    cen = t - mu
    var = jnp.mean(cen * cen, axis=-1, keepdims=True)
    v = (cen * lax.rsqrt(var + NORM_EPS) * ng_ref[...]).astype(BF16)
    group = lax.broadcasted_iota(jnp.int32, (1, 2 * V7X_LANES), 1) // SG_GROUP_DIM
    bias = bm_ref[...]
    for c in range(tm // SG_CHUNK):
        vc = v[c * SG_CHUNK:(c + 1) * SG_CHUNK, :]
        s = bias
        for g in range(SG_GROUPS):
            s = s + jnp.where(group == g, _dot(w_ref[g], vc), 0.0)
        o_ref[c * SG_CHUNK:(c + 1) * SG_CHUNK, :] = (u[c * SG_CHUNK:(c + 1) * SG_CHUNK, :] * s).astype(BF16)


def spatial_gating(z, ng, w, bm, tm):
    b, r, _ = z.shape
    wide = 2 * V7X_LANES
    return pl.pallas_call(
        _sg_body,
        grid=(b, r // tm),
        in_specs=[pl.BlockSpec((None, tm, wide), lambda i, j: (i, j, COL_SU256)),
                  pl.BlockSpec((None, tm, wide), lambda i, j: (i, j, COL_SV256)),
                  pl.BlockSpec((1, wide), lambda i, j: (0, 0)),
                  pl.BlockSpec((SG_GROUPS, SG_CHUNK, SG_CHUNK), lambda i, j: (0, 0, 0)),
                  pl.BlockSpec((SG_CHUNK, wide), lambda i, j: (0, 0))],
        out_specs=pl.BlockSpec((None, tm, wide), lambda i, j: (i, j, 0)),
        out_shape=jax.ShapeDtypeStruct((b, r, wide), BF16),
        compiler_params=_cparams(("parallel", "parallel"), 32),
        name="spatial_gating",
    )(z, z, ng, w, bm)


def _out_proj_body(a_ref, yf_ref, yb_ref, sg_ref, x_ref, g1_ref, a2_ref, s2_ref, w_ref, rwt_ref,
                   xo_ref, f_ref, aff_ref):
    na = a_ref.shape[1]
    nr = yf_ref.shape[1]
    tm = x_ref.shape[0]
    part = min(tm, 256)
    parts = [slice(s, s + part) for s in range(0, tm, part)]
    ys = []
    for rows in parts:
        r = (yf_ref[rows, :] + yb_ref[rows, :]).astype(BF16)
        ys.append(_dot(a_ref[rows, :], w_ref[0:na, :]) + _dot(r, w_ref[na:na + nr, :])
                  + _dot(sg_ref[rows, :], w_ref[na + nr:, :]))
    for rows, y in zip(parts, ys):
        xn = x_ref[rows, :] + g1_ref[...] * y
        xo_ref[rows, :] = xn
        ms = jnp.mean(xn * xn, axis=-1, keepdims=True)
        f = (xn * lax.rsqrt(ms + NORM_EPS) * a2_ref[...] + s2_ref[...]).astype(BF16)
        f_ref[rows, :] = f
        logits = _dot_nt(rwt_ref[...], f)
        e = jnp.exp(logits - jnp.max(logits, axis=0, keepdims=True))
        aff_ref[:, rows] = e / jnp.sum(e, axis=0, keepdims=True)


def out_proj(a, yf, yb, sg, x, g1, a2, s2, w, rwt, tm):
    b, r, d = x.shape
    rows = lambda width: pl.BlockSpec((None, tm, width), lambda i, j: (i, j, 0))
    vec = pl.BlockSpec((None, 1, d), lambda i, j: (i, 0, 0))
    return pl.pallas_call(
        _out_proj_body,
        grid=(b, r // tm),
        in_specs=[rows(a.shape[2]), rows(yf.shape[2]), rows(yb.shape[2]), rows(sg.shape[2]), rows(d),
                  vec, vec, vec,
                  pl.BlockSpec(w.shape, lambda i, j: (0, 0)),
                  pl.BlockSpec(rwt.shape, lambda i, j: (0, 0))],
        out_specs=[rows(d), rows(d), pl.BlockSpec((None, N_EXPERTS, tm), lambda i, j: (i, 0, j))],
        out_shape=[jax.ShapeDtypeStruct((b, r, d), F32), jax.ShapeDtypeStruct((b, r, d), BF16),
                   jax.ShapeDtypeStruct((b, N_EXPERTS, r), F32)],
        compiler_params=_cparams(("parallel", "parallel"), 48),
        name="out_proj",
    )(a, yf, yb, sg, x, g1, a2, s2, w, rwt)


def _router_body(aff_ref, tri_ref, posm_ref, off_ref, *, cap, tb):
    aff = aff_ref[...]
    e, r = aff.shape
    nb = r // tb
    bits = pltpu.bitcast(aff, jnp.int32)

    def count_at_least(x, t):
        return jnp.sum(jnp.where(x >= t, 1.0, 0.0), axis=1, keepdims=True)

    def coarse(i, ebits):
        cand = ebits | lax.shift_left(jnp.int32(1), 30 - i)
        return jnp.where(count_at_least(bits, cand) >= cap, cand, ebits)

    base = pltpu.bitcast(lax.fori_loop(0, 8, coarse, jnp.zeros((e, 1), jnp.int32)), F32)

    def fine(i, bracket):
        lo, hi = bracket
        mid = 0.5 * (lo + hi)
        keep = count_at_least(aff, mid) >= cap
        return jnp.where(keep, mid, lo), jnp.where(keep, hi, mid)

    lo, hi = lax.fori_loop(0, 32, fine, (0.5 * base, jnp.maximum(4.0 * base, 2.0 ** -124)))
    gt = jnp.where(aff >= hi, 1.0, 0.0)
    eq = jnp.where(aff >= lo, 1.0, 0.0) - gt
    need = cap - jnp.sum(gt, axis=1, keepdims=True)
    tri = tri_ref[...]
    lane = lax.broadcasted_iota(jnp.int32, (e, V7X_LANES), 1)
    offs = jnp.zeros((e, V7X_LANES), jnp.int32)
    run_eq = jnp.zeros((e, 1), F32)
    run_sel = jnp.zeros((e, 1), F32)
    for j in range(nb):
        cols = slice(j * tb, (j + 1) * tb)
        eq_j = eq[:, cols]
        before_eq = _dot(eq_j.astype(BF16), tri) + run_eq
        sel_j = gt[:, cols] + eq_j * jnp.where(before_eq < need, 1.0, 0.0)
        pos_j = _dot(sel_j.astype(BF16), tri) + run_sel
        posm_ref[:, cols] = jnp.where(sel_j > 0.0, pos_j, -1.0)
        offs = jnp.where(lane == j, run_sel.astype(jnp.int32), offs)
        run_eq = run_eq + jnp.sum(eq_j, axis=1, keepdims=True)
        run_sel = run_sel + jnp.sum(sel_j, axis=1, keepdims=True)
    off_ref[...] = jnp.where(lane == nb, run_sel.astype(jnp.int32), offs)


def route(aff_t, tri, cap):
    b, e, r = aff_t.shape
    tb = tri.shape[0]
    posm, off = pl.pallas_call(
        functools.partial(_router_body, cap=cap, tb=tb),
        grid=(b,),
        in_specs=[pl.BlockSpec((None, e, r), lambda i: (i, 0, 0)),
                  pl.BlockSpec((tb, tb), lambda i: (0, 0))],
        out_specs=[pl.BlockSpec((None, e, r), lambda i: (i, 0, 0)),
                   pl.BlockSpec((None, e, V7X_LANES), lambda i: (i, 0, 0))],
        out_shape=[jax.ShapeDtypeStruct((b, e, r), F32), jax.ShapeDtypeStruct((b, e, V7X_LANES), jnp.int32)],
        compiler_params=_cparams(("parallel",), 32),
        name="router",
    )(aff_t, tri)
    shape5 = (b, e, r // tb, 1, tb)
    return posm.reshape(shape5), aff_t.reshape(shape5), off.reshape(-1)


def _gather_body(off_ref, f_ref, posm_ref, aff_ref, x_hbm, g_hbm, xall, gall, pcat, sem, *, cap, w, nb, n_exp):
    b = pl.program_id(0)
    j = pl.program_id(1)
    align = V7X_BF16_SUBLANES

    @pl.when(j == 0)
    def _():
        xall[...] = jnp.zeros(xall.shape, BF16)
        gall[...] = jnp.zeros(gall.shape, F32)

    iota = lax.broadcasted_iota(jnp.int32, (w, 1), 0).astype(F32)

    def window(e, rnd):
        off = off_ref[(b * n_exp + e) * V7X_LANES + j]
        return pl.multiple_of(jnp.minimum((off // align) * align + rnd * w, cap), align)

    def one_round(rnd):
        for e in range(n_exp):
            first = window(e, rnd)
            onehot = posm_ref[e] == first.astype(F32) + iota
            pcat[e * w:(e + 1) * w, :] = jnp.where(onehot, 1.0, 0.0).astype(BF16)
            gall[e, pl.ds(first, w), :] += jnp.sum(jnp.where(onehot, aff_ref[e], 0.0), axis=1, keepdims=True)
        rows = _dot(pcat[...], f_ref[...])
        for e in range(n_exp):
            xall[e, pl.ds(window(e, rnd), w), :] += rows[e * w:(e + 1) * w, :].astype(BF16)

    one_round(0)
    span = jnp.int32(0)
    for e in range(n_exp):
        idx = (b * n_exp + e) * V7X_LANES + j
        span = jnp.maximum(span, off_ref[idx + 1] - (off_ref[idx] // align) * align)

    def extra(rnd, carry):
        one_round(rnd)
        return carry

    lax.fori_loop(1, (span + w - 1) // w, extra, 0)

    @pl.when(j == nb - 1)
    def _():
        cx = pltpu.make_async_copy(xall.at[:, pl.ds(0, cap), :], x_hbm.at[b], sem.at[0])
        cg = pltpu.make_async_copy(gall.at[:, pl.ds(0, cap), :], g_hbm.at[b], sem.at[1])
        cx.start()
        cg.start()
        cx.wait()
        cg.wait()


def moe_gather(off, f, posm, aff, cap, w):
    b, r, d = f.shape
    _, e, nb, _, tb = posm.shape
    blk = pl.BlockSpec((None, e, None, 1, tb), lambda i, j, o: (i, 0, j, 0, 0))
    return pl.pallas_call(
        functools.partial(_gather_body, cap=cap, w=w, nb=nb, n_exp=e),
        grid_spec=pltpu.PrefetchScalarGridSpec(
            num_scalar_prefetch=1,
            grid=(b, nb),
            in_specs=[pl.BlockSpec((None, tb, d), lambda i, j, o: (i, j, 0)), blk, blk],
            out_specs=[pl.BlockSpec(memory_space=pl.ANY), pl.BlockSpec(memory_space=pl.ANY)],
            scratch_shapes=[pltpu.VMEM((e, cap + w, d), BF16), pltpu.VMEM((e, cap + w, 1), F32),
                            pltpu.VMEM((e * w, tb), BF16), pltpu.SemaphoreType.DMA((2,))]),
        out_shape=[jax.ShapeDtypeStruct((b, e, cap, d), BF16), jax.ShapeDtypeStruct((b, e, cap, 1), F32)],
        compiler_params=_cparams(("arbitrary", "arbitrary"), 56),
        name="moe_gather",
    )(off, f, posm, aff)


def _moe_ffn_body(*refs, has_ctx, cap, capc):
    if has_ctx:
        _, x_ref, g_ref, xc_ref, gc_ref, wg_ref, wu_ref, wd_ref, y_ref, yc_ref, yacc, ycacc = refs
    else:
        _, x_ref, g_ref, wg_ref, wu_ref, wd_ref, y_ref, yacc = refs
    c = pl.program_id(1)
    last = pl.num_programs(1) - 1
    nbatch, _, d = x_ref.shape
    wg = wg_ref[...].astype(BF16)
    wu = wu_ref[...].astype(BF16)
    wd = wd_ref[...].astype(BF16)

    def hidden(x):
        a = _dot(x, wg)
        u = _dot(x, wu)
        return (a * _sigmoid(a) * u).astype(BF16)

    @pl.when(c == 0)
    def _():
        yacc[...] = jnp.zeros(yacc.shape, F32)
        if has_ctx:
            ycacc[...] = jnp.zeros(ycacc.shape, F32)

    xs = [x_ref[i] for i in range(nbatch)]
    if has_ctx:
        xs[-1] = jnp.concatenate([xs[-1], xc_ref[...].reshape(nbatch * capc, d)], axis=0)
    hs = [hidden(x) for x in xs]
    ys = [_dot(h, wd) for h in hs]
    for i in range(nbatch):
        yacc[i] += ys[i][0:cap]
    if has_ctx:
        ycacc[...] += ys[-1][cap:]

    @pl.when(c == last)
    def _():
        for i in range(nbatch):
            y_ref[i, 0:cap, :] = (yacc[i] * g_ref[i]).astype(BF16)
            y_ref[i, cap:, :] = jnp.zeros((y_ref.shape[1] - cap, d), BF16)
        if has_ctx:
            yc = ycacc[...] * gc_ref[...].reshape(nbatch * capc, 1)
            yc_ref[:, 0:capc, :] = yc.reshape(nbatch, capc, d).astype(BF16)
            yc_ref[:, capc:, :] = jnp.zeros((nbatch, yc_ref.shape[1] - capc, d), BF16)


def moe_ffn(layer, x, g, xc, gc, wg, wu, wd, tf, pad, padc):
    b, e, cap, d = x.shape
    ff = wg.shape[3]
    has_ctx = xc is not None
    capc = xc.shape[2] if has_ctx else 0
    tok = lambda rows, width: pl.BlockSpec((b, None, rows, width), lambda j, k, l: (0, j, 0, 0))
    in_specs = [tok(cap, d), tok(cap, 1)]
    out_specs = [tok(cap + pad, d)]
    out_shape = [jax.ShapeDtypeStruct((b, e, cap + pad, d), BF16)]
    scratch = [pltpu.VMEM((b, cap, d), F32)]
    args = [x, g]
    if has_ctx:
        in_specs += [tok(capc, d), tok(capc, 1)]
        out_specs.append(tok(capc + padc, d))
        out_shape.append(jax.ShapeDtypeStruct((b, e, capc + padc, d), BF16))
        scratch.append(pltpu.VMEM((b * capc, d), F32))
        args += [xc, gc]
    in_specs += [pl.BlockSpec((None, None, d, tf), lambda j, k, l: (l[0], j, 0, k)),
                 pl.BlockSpec((None, None, d, tf), lambda j, k, l: (l[0], j, 0, k)),
                 pl.BlockSpec((None, None, tf, d), lambda j, k, l: (l[0], j, k, 0))]
    out = pl.pallas_call(
        functools.partial(_moe_ffn_body, has_ctx=has_ctx, cap=cap, capc=capc),
        grid_spec=pltpu.PrefetchScalarGridSpec(
            num_scalar_prefetch=1, grid=(e, ff // tf),
            in_specs=in_specs, out_specs=out_specs, scratch_shapes=scratch),
        out_shape=out_shape,
        compiler_params=_cparams(("parallel", "arbitrary"), 56),
        name="moe_ffn",
    )(layer, *args, wg, wu, wd)
    return out if has_ctx else (out[0], None)


def _scatter_body(off_ref, x_ref, g2_ref, posm_ref, y_hbm, xo_ref, ybuf, pcat, sem, *, cap, w, nb, n_exp):
    b = pl.program_id(0)
    j = pl.program_id(1)
    step = b * nb + j
    slot = lax.rem(step, 2)
    spare = 2
    align = V7X_BF16_SUBLANES

    def window(bb, jj, e, rnd):
        off = off_ref[(bb * n_exp + e) * V7X_LANES + jj]
        return jnp.minimum((off // align) * align + rnd * w, cap)

    def copies(bb, jj, buf, rnd):
        return [pltpu.make_async_copy(
            y_hbm.at[bb, e, pl.ds(pl.multiple_of(window(bb, jj, e, rnd), align), w), :],
            ybuf.at[buf, pl.ds(e * w, w), :], sem.at[buf]) for e in range(n_exp)]

    @pl.when(step == 0)
    def _():
        for cp in copies(b, j, 0, 0):
            cp.start()

    @pl.when(step + 1 < pl.num_programs(0) * nb)
    def _():
        nxt = step + 1
        for cp in copies(nxt // nb, lax.rem(nxt, nb), 1 - slot, 0):
            cp.start()

    iota = lax.broadcasted_iota(jnp.int32, (w, 1), 0).astype(F32)

    def onehots(rnd):
        for e in range(n_exp):
            first = window(b, j, e, rnd).astype(F32)
            pcat[e * w:(e + 1) * w, :] = jnp.where(posm_ref[e] == first + iota, 1.0, 0.0).astype(BF16)

    for cp in copies(b, j, slot, 0):
        cp.wait()
    onehots(0)
    g2 = g2_ref[...]
    xo_ref[...] = x_ref[...] + g2 * _dot_tn(pcat[...], ybuf[slot])

    span = jnp.int32(0)
    for e in range(n_exp):
        idx = (b * n_exp + e) * V7X_LANES + j
        span = jnp.maximum(span, off_ref[idx + 1] - (off_ref[idx] // align) * align)

    def extra(rnd, carry):
        for cp in copies(b, j, spare, rnd):
            cp.start()
        for cp in copies(b, j, spare, rnd):
            cp.wait()
        onehots(rnd)
        xo_ref[...] += g2 * _dot_tn(pcat[...], ybuf[spare])
        return carry

    lax.fori_loop(1, (span + w - 1) // w, extra, 0)


def moe_scatter(off, x, g2, posm, y, cap, w):
    b, r, d = x.shape
    _, e, nb, _, tb = posm.shape
    rows = pl.BlockSpec((None, tb, d), lambda i, j, o: (i, j, 0))
    return pl.pallas_call(
        functools.partial(_scatter_body, cap=cap, w=w, nb=nb, n_exp=e),
        grid_spec=pltpu.PrefetchScalarGridSpec(
            num_scalar_prefetch=1,
            grid=(b, nb),
            in_specs=[rows, pl.BlockSpec((None, 1, d), lambda i, j, o: (i, 0, 0)),
                      pl.BlockSpec((None, e, None, 1, tb), lambda i, j, o: (i, 0, j, 0, 0)),
                      pl.BlockSpec(memory_space=pl.ANY)],
            out_specs=rows,
            scratch_shapes=[pltpu.VMEM((3, e * w, d), BF16), pltpu.VMEM((e * w, tb), BF16),
                            pltpu.SemaphoreType.DMA((3,))]),
        out_shape=jax.ShapeDtypeStruct((b, r, d), F32),
        compiler_params=_cparams(("arbitrary", "arbitrary"), 32),
        name="moe_scatter",
    )(off, x, g2, posm, y)


def _attn_tables(n):
    rows = n // GRID_W
    row = jnp.broadcast_to(jnp.arange(rows, dtype=F32)[:, None], (rows, GRID_W)).reshape(-1)
    col = jnp.broadcast_to(jnp.arange(GRID_W, dtype=F32)[None, :], (rows, GRID_W)).reshape(-1)
    quarter = DA_HEAD_DIM // 4
    inv = ROPE_BASE ** (-jnp.arange(quarter, dtype=F32) / quarter)
    ang_r, ang_c = row[:, None] * inv, col[:, None] * inv

    def half(ang):
        return (jnp.concatenate([jnp.cos(ang), jnp.cos(ang)], -1),
                jnp.concatenate([-jnp.sin(ang), jnp.sin(ang)], -1))

    cr, sr = half(ang_r)
    cc, sc = half(ang_c)
    cos64 = jnp.concatenate([cr, cc], -1)
    sin64 = jnp.concatenate([sr, sc], -1)
    return jnp.tile(cos64, (1, 2)), jnp.tile(sin64, (1, 2))


def _ret_tables(n):
    inv = 1.0 / (ROPE_BASE ** jnp.linspace(0.0, 1.0, RET_QK_DIM // 2, dtype=F32))
    ang = jnp.arange(n, dtype=F32)[:, None] * inv
    cos64 = jnp.concatenate([jnp.cos(ang), jnp.cos(ang)], -1)
    sin64 = jnp.concatenate([-jnp.sin(ang), jnp.sin(ang)], -1)
    return jnp.tile(cos64, (1, 2)), jnp.tile(sin64, (1, 2))


def _ret_decay_tables(log_gamma, lc):
    pos = jnp.arange(lc, dtype=F32)
    lg = jnp.repeat(log_gamma, RET_QK_DIM, axis=1).reshape(2, 2, 1, V7X_LANES)
    qd = jnp.stack([jnp.exp(lg[0] * (pos + 1.0)[None, :, None]), jnp.exp(lg[1] * (lc - pos)[None, :, None])])
    kd = jnp.stack([jnp.exp(lg[0] * (lc - 1.0 - pos)[None, :, None]), jnp.exp(lg[1] * pos[None, :, None])])
    cd = jnp.broadcast_to(jnp.exp(lg * lc), (2, 2, lc, V7X_LANES))
    dec = jnp.stack([qd, kd, cd], axis=2)
    dist = pos[:, None] - pos[None, :]
    lgh = log_gamma[:, :, None, None]
    fwd = jnp.where(dist >= 0, jnp.exp(lgh[0] * jnp.maximum(dist, 0.0)), 0.0)
    bwd = jnp.where(dist <= 0, jnp.exp(lgh[1] * jnp.maximum(-dist, 0.0)), 0.0)
    intra = jnp.stack([fwd, bwd]).reshape(2, 2, 2, lc, lc)
    return jnp.swapaxes(dec, 0, 1), jnp.swapaxes(intra, 0, 1)


def kernel(x, c, ctx, c_ctx, w_mod, b_mod, norm1_g, norm2_g, w_in, w_out, da_q_norm_g, da_k_norm_g,
           da_lambda_q1, da_lambda_k1, da_lambda_q2, da_lambda_k2, da_subln_g, ret_decay,
           sg_norm_g, sg_w, sg_b, router_w, ex_w_gate, ex_w_up, ex_w_down):
    b, n, d = x.shape
    n_ctx = ctx.shape[1]
    cos_a, sin_a = _attn_tables(n)
    cos_r, sin_r = _ret_tables(n)
    one_c = jnp.ones((n_ctx, V7X_LANES), F32)
    zero_c = jnp.zeros((n_ctx, V7X_LANES), F32)
    lane = jnp.arange(V7X_LANES)
    seg = (lane[:, None] // 64 == lane[None, :] // 64).astype(BF16)
    state0 = jnp.zeros((b, 2, 2, V7X_LANES, V7X_LANES), F32)
    tk = n_ctx
    assert (n + n_ctx) // tk % ATTN_BLOCKS_PER_STEP == 0
    tok = jnp.arange(MOE_TOKEN_BLOCK)
    tri = (tok[:, None] < tok[None, :]).astype(BF16)
    cap = EC_CAPACITY * n // N_EXPERTS
    cap_c = EC_CAPACITY * n_ctx // N_EXPERTS
    pad, pad_c = min(64, cap), min(64, cap_c)

    mods = modulation(jnp.concatenate([c, c_ctx[None, :]], axis=0), w_mod, b_mod)

    for layer in range(DEPTH):
        need_ctx = layer < DEPTH - 1
        lambda_init = 0.8 - 0.6 * math.exp(-0.3 * layer)
        mod = mods[layer, 0:b]
        mod_c = jnp.broadcast_to(mods[layer, b], mod.shape)
        sh1, sc1, g1, sh2, sc2, g2 = jnp.split(mod[:, None, :], 6, axis=-1)
        csh1, csc1, cg1, csh2, csc2, cg2 = jnp.split(mod_c[:, None, :], 6, axis=-1)
        lam = (jnp.exp(jnp.sum(da_lambda_q1[layer] * da_lambda_k1[layer]))
               - jnp.exp(jnp.sum(da_lambda_q2[layer] * da_lambda_k2[layer])) + lambda_init).reshape(1)
        log_gamma = jax.nn.log_sigmoid(ret_decay[layer].astype(F32))
        dec, intra = _ret_decay_tables(log_gamma, RET_SCAN_CHUNK)
        w_in_b = w_in[layer].astype(BF16)
        w_out_b = w_out[layer].astype(BF16)
        rwt = router_w[layer].T.astype(BF16)
        qg = jnp.tile(da_q_norm_g[layer], 2)[None, :]
        kg = jnp.tile(da_k_norm_g[layer], 2)[None, :]
        gcol = (da_subln_g[layer] * (1.0 - lambda_init))[:, None]
        sg_ng = sg_norm_g[layer][None, :]
        sg_wb = sg_w[layer].astype(BF16)
        sg_bm = jnp.repeat(sg_b[layer].T, SG_GROUP_DIM, axis=1)
        n1, n2 = norm1_g[layer][None, None, :], norm2_g[layer][None, None, :]

        z_lat = in_proj(x, n1 * (1.0 + sc1), sh1, w_in_b, 512)
        z_ctx = in_proj(ctx, n1 * (1.0 + csc1), csh1, w_in_b, n_ctx)

        cq1, cq2, ck, cvt = qkv_prep(z_ctx, one_c, zero_c, qg, kg, seg, n_ctx, tk)
        q1, q2, kl, vlt = qkv_prep(z_lat, cos_a, sin_a, qg, kg, seg, _row_tile(n, 1024), tk)
        sources = [(ck, cvt), (kl, vlt)]
        bound = score_bound(da_q_norm_g[layer], da_k_norm_g[layer])
        c1, c2 = softmax_shifts(bound, q1, q2, [ck, kl], ATTN_QUERY_TILE, tk)
        a_lat = diff_attention(lam, q1, q2, c1, c2, sources, gcol, ATTN_QUERY_TILE, ATTN_BLOCKS_PER_STEP)

        cyf, cyb, cstate = retention(z_ctx, one_c, zero_c, dec, intra, seg, state0, n_ctx)
        yf, yb, _ = retention(z_lat, cos_r, sin_r, dec, intra, seg, cstate, _row_tile(n, 1024))

        s_lat = spatial_gating(z_lat, sg_ng, sg_wb, sg_bm, 512)

        x, f_lat, aff_lat = out_proj(a_lat, yf, yb, s_lat, x, g1, n2 * (1.0 + sc2), sh2, w_out_b, rwt,
                                     _row_tile(n, 1024))
        posm, aff5, off = route(aff_lat, tri, cap)
        xs, gs = moe_gather(off, f_lat, posm, aff5, cap, pad)
        xcs = gcs = None
        if need_ctx:
            cc1, cc2 = softmax_shifts(bound, cq1, cq2, [ck], n_ctx, tk)
            a_ctx = diff_attention(lam, cq1, cq2, cc1, cc2, sources[:1], gcol, n_ctx, 1)
            s_ctx = spatial_gating(z_ctx, sg_ng, sg_wb, sg_bm, n_ctx)
            ctx, f_ctx, aff_ctx = out_proj(a_ctx, cyf, cyb, s_ctx, ctx, cg1, n2 * (1.0 + csc2), csh2,
                                           w_out_b, rwt, n_ctx)
            posm_c, aff5_c, off_c = route(aff_ctx, tri, cap_c)
            xcs, gcs = moe_gather(off_c, f_ctx, posm_c, aff5_c, cap_c, pad_c)
        ys, ycs = moe_ffn(jnp.full((1,), layer, jnp.int32), xs, gs, xcs, gcs, ex_w_gate, ex_w_up, ex_w_down,
                          512, pad, pad_c)
        x = moe_scatter(off, x, g2, posm, ys, cap, pad)
        if need_ctx:
            ctx = moe_scatter(off_c, ctx, cg2, posm_c, ycs, cap_c, pad_c)
    return x
```

```python
import functools
import math

import jax
import jax.numpy as jnp
from jax import lax
from jax.experimental import pallas as pl
from jax.experimental.pallas import tpu as pltpu

F32 = jnp.float32
BF16 = jnp.bfloat16

D_MODEL = 1024
DEPTH = 4
GRID_W = 64
NORM_EPS = 1e-6
ROPE_BASE = 10000.0
DA_HEADS = 4
DA_HEAD_DIM = 64
RET_HEADS = 4
RET_QK_DIM = 64
RET_SCAN_CHUNK = 128
SG_GROUPS = 4
SG_GROUP_DIM = 64
SG_CHUNK = 128
N_EXPERTS = 16
EXPERT_FF = 2 * D_MODEL
EC_CAPACITY = 2
IN_COLS = 3328
COL_AQ, COL_AK, COL_AV = 0, 4, 8
COL_RQ, COL_RK, COL_RV, COL_RGF, COL_RGB = 12, 14, 16, 18, 20
COL_SU256, COL_SV256 = 11, 12

V7X_LANES = 128
V7X_BF16_SUBLANES = 16
V7X_VMEM_BYTES = 64 * 1024 * 1024
MOE_TOKEN_BLOCK = 256
ATTN_BLOCKS_PER_STEP = 11
ATTN_QUERY_TILE = 512
NEG_BIG = -1e30
F32_EXP_RANGE = 83.0


def _cparams(semantics, vmem_mib):
    assert vmem_mib * 1024 * 1024 < V7X_VMEM_BYTES
    return pltpu.CompilerParams(dimension_semantics=semantics, vmem_limit_bytes=vmem_mib * 1024 * 1024)


def _row_tile(rows, preferred):
    while rows % preferred:
        preferred //= 2
    return preferred


def _dot(a, b):
    return jnp.dot(a, b, preferred_element_type=F32)


def _dot_nt(a, b):
    return lax.dot_general(a, b, (((1,), (1,)), ((), ())), preferred_element_type=F32)


def _dot_tn(a, b):
    return lax.dot_general(a, b, (((0,), (0,)), ((), ())), preferred_element_type=F32)


def _seg_dot(x, seg):
    hi = x.astype(BF16)
    lo = (x - hi.astype(F32)).astype(BF16)
    return _dot(hi, seg) + _dot(lo, seg)


def _sigmoid(x):
    return 1.0 / (1.0 + jnp.exp(-x))


def _modulation_body(c_ref, w_ref, b_ref, o_ref):
    cv = c_ref[...]
    o_ref[...] = _dot((cv * _sigmoid(cv)).astype(BF16), w_ref[...].astype(BF16)) + b_ref[...]


def modulation(cvecs, w_mod, b_mod):
    depth, d, cols = w_mod.shape
    rows = -(-cvecs.shape[0] // 8) * 8
    cpad = jnp.concatenate([cvecs, jnp.zeros((rows - cvecs.shape[0], d), F32)], axis=0)
    tn = cols // 4
    return pl.pallas_call(
        _modulation_body,
        grid=(depth, cols // tn),
        in_specs=[pl.BlockSpec((rows, d), lambda l, j: (0, 0)),
                  pl.BlockSpec((None, d, tn), lambda l, j: (l, 0, j)),
                  pl.BlockSpec((None, 1, tn), lambda l, j: (l, 0, j))],
        out_specs=pl.BlockSpec((None, rows, tn), lambda l, j: (l, 0, j)),
        out_shape=jax.ShapeDtypeStruct((depth, rows, cols), F32),
        compiler_params=_cparams(("parallel", "parallel"), 32),
        name="modulation",
    )(cpad, w_mod, b_mod[:, None, :])


def _in_proj_body(x_ref, a_ref, s_ref, w_ref, z_ref):
    x = x_ref[...]
    ms = jnp.mean(x * x, axis=-1, keepdims=True)
    h = x * lax.rsqrt(ms + NORM_EPS) * a_ref[...] + s_ref[...]
    z_ref[...] = _dot(h.astype(BF16), w_ref[...])


def in_proj(x, a, s, w, tm):
    b, r, d = x.shape
    c = w.shape[1]
    return pl.pallas_call(
        _in_proj_body,
        grid=(b, r // tm),
        in_specs=[
            pl.BlockSpec((None, tm, d), lambda i, j: (i, j, 0)),
            pl.BlockSpec((None, 1, d), lambda i, j: (i, 0, 0)),
            pl.BlockSpec((None, 1, d), lambda i, j: (i, 0, 0)),
            pl.BlockSpec((d, c), lambda i, j: (0, 0)),
        ],
        out_specs=pl.BlockSpec((None, tm, c), lambda i, j: (i, j, 0)),
        out_shape=jax.ShapeDtypeStruct((b, r, c), F32),
        compiler_params=_cparams(("parallel", "parallel"), 48),
        name="in_proj",
    )(x, a, s, w)


def _qkv_prep_body(q_ref, k_ref, v_ref, cos_ref, sin_ref, qg_ref, kg_ref, seg_ref,
                   qp1_ref, qp2_ref, kk_ref, vt_ref):
    lane = lax.broadcasted_iota(jnp.int32, (1, V7X_LANES), 1)
    lower16 = (lane % 32) < 16
    seg = seg_ref[...]
    cos = cos_ref[...]
    sin = sin_ref[...]

    def norm_rope(x, g):
        ms = _seg_dot(x * x, seg) * (1.0 / DA_HEAD_DIM)
        y = x * lax.rsqrt(ms + NORM_EPS) * g
        partner = jnp.where(lower16, pltpu.roll(y, 112, 1), pltpu.roll(y, 16, 1))
        return y * cos + partner * sin

    q = norm_rope(q_ref[...], qg_ref[...]) * DA_HEAD_DIM ** -0.5
    first = lane < DA_HEAD_DIM
    qp1_ref[...] = jnp.where(first, q, 0.0).astype(BF16)
    qp2_ref[...] = jnp.where(first, 0.0, q).astype(BF16)
    kk_ref[...] = norm_rope(k_ref[...], kg_ref[...]).astype(BF16)
    tv = vt_ref.shape[2]
    for s in range(vt_ref.shape[0]):
        vt_ref[s] = v_ref[s * tv:(s + 1) * tv, :].T.astype(BF16)


def qkv_prep(z, cos, sin, qg, kg, seg, tm, tv):
    b, r, _ = z.shape
    h = DA_HEADS
    row = lambda off: pl.BlockSpec((None, tm, V7X_LANES), lambda i, j, k: (i, k, off + j))
    tab = pl.BlockSpec((tm, V7X_LANES), lambda i, j, k: (k, 0))
    vec = pl.BlockSpec((1, V7X_LANES), lambda i, j, k: (0, 0))
    o4 = pl.BlockSpec((None, None, tm, V7X_LANES), lambda i, j, k: (i, j, k, 0))
    sd = jax.ShapeDtypeStruct((b, h, r, V7X_LANES), BF16)
    return pl.pallas_call(
        _qkv_prep_body,
        grid=(b, h, r // tm),
        in_specs=[row(COL_AQ), row(COL_AK), row(COL_AV), tab, tab, vec, vec,
                  pl.BlockSpec((V7X_LANES, V7X_LANES), lambda i, j, k: (0, 0))],
        out_specs=[o4, o4, o4,
                   pl.BlockSpec((None, None, tm // tv, V7X_LANES, tv), lambda i, j, k: (i, j, k, 0, 0))],
        out_shape=[sd, sd, sd, jax.ShapeDtypeStruct((b, h, r // tv, V7X_LANES, tv), BF16)],
        compiler_params=_cparams(("parallel", "parallel", "parallel"), 32),
        name="qkv_prep",
    )(z, z, z, cos, sin, qg, kg, seg)


def _attn_body(*refs, bps, n_src, shift):
    lam_ref, q1_ref, q2_ref, c1_ref, c2_ref = refs[:5]
    srcs = [(refs[5 + 2 * s], refs[6 + 2 * s]) for s in range(n_src)]
    g_ref, o_ref, acc_ref = refs[5 + 2 * n_src:]
    q1 = q1_ref[...]
    q2 = q2_ref[...]
    c1 = c1_ref[...]
    c2 = c2_ref[...]
    tq = q1.shape[0]
    tk = srcs[0][1].shape[2]
    blocks = [(s, j) for s, (_, v_ref) in enumerate(srcs) for j in range(v_ref.shape[0])]
    n_steps = len(blocks) // bps

    def runs(i):
        out = []
        for s, j in blocks[i * bps:(i + 1) * bps]:
            if out and out[-1][0] == s and out[-1][2] == j:
                out[-1][2] = j + 1
            else:
                out.append([s, j, j + 1])
        return out

    def probs(i):
        segs, d1, d2 = [], 0.0, 0.0
        for s, lo, hi in runs(i):
            k = srcs[s][0][lo * tk:hi * tk, :]
            s1 = _dot_nt(k, q1)
            s2 = _dot_nt(k, q2)
            p1 = jnp.exp(s1 - c1) if shift else jnp.exp(s1)
            p2 = jnp.exp(s2 - c2) if shift else jnp.exp(s2)
            segs.append((s, lo, hi, (p1.astype(BF16), p2.astype(BF16))))
            d1 = d1 + jnp.sum(p1.reshape(-1, 8, tq), axis=0)
            d2 = d2 + jnp.sum(p2.reshape(-1, 8, tq), axis=0)
        return segs, d1, d2

    def accumulate(segs):
        for idx in range(2):
            part = None
            for s, lo, hi, ps in segs:
                for j in range(lo, hi):
                    term = _dot(srcs[s][1][j], ps[idx][(j - lo) * tk:(j - lo + 1) * tk])
                    part = term if part is None else part + term
            acc_ref[idx] += part

    acc_ref[...] = jnp.zeros(acc_ref.shape, F32)
    segs, l1, l2 = probs(0)
    for i in range(n_steps - 1):
        nxt, d1, d2 = probs(i + 1)
        accumulate(segs)
        segs, l1, l2 = nxt, l1 + d1, l2 + d2
    accumulate(segs)
    l1 = jnp.sum(l1, axis=0, keepdims=True)
    l2 = jnp.sum(l2, axis=0, keepdims=True)
    o = acc_ref[0] / l1 - lam_ref[0] * (acc_ref[1] / l2)
    ms = jnp.mean(o * o, axis=0, keepdims=True)
    y = o * lax.rsqrt(ms + NORM_EPS) * g_ref[...]
    o_ref[...] = y.T.astype(BF16)


def diff_attention(lam, q1, q2, c1, c2, sources, gcol, tq, bps, shift):
    b, h, nq, _ = q1.shape
    qs = pl.BlockSpec((None, None, tq, V7X_LANES), lambda i, j, s: (i, j, s, 0))
    cs = pl.BlockSpec((None, None, 1, tq), lambda i, j, s: (i, j, 0, s))
    in_specs = [pl.BlockSpec(memory_space=pltpu.SMEM), qs, qs, cs, cs]
    args = [lam, q1, q2, c1, c2]
    for k, vt in sources:
        in_specs += [pl.BlockSpec((None, None) + k.shape[2:], lambda i, j, s: (i, j, 0, 0)),
                     pl.BlockSpec((None, None) + vt.shape[2:], lambda i, j, s: (i, j, 0, 0, 0))]
        args += [k, vt]
    assert sum(vt.shape[2] for _, vt in sources) % bps == 0
    return pl.pallas_call(
        functools.partial(_attn_body, bps=bps, n_src=len(sources), shift=shift),
        grid=(b, h, nq // tq),
        in_specs=in_specs + [pl.BlockSpec((V7X_LANES, 1), lambda i, j, s: (0, 0))],
        out_specs=pl.BlockSpec((None, tq, V7X_LANES), lambda i, j, s: (i, s, j)),
        out_shape=jax.ShapeDtypeStruct((b, nq, h * V7X_LANES), BF16),
        scratch_shapes=[pltpu.VMEM((2, V7X_LANES, tq), F32)],
        compiler_params=_cparams(("parallel", "parallel", "parallel"), 48),
        name="diff_attention",
    )(*args, gcol)


def _rowmax_body(q1_ref, q2_ref, k_ref, m1_ref, m2_ref, *, n_blocks, tk):
    q1 = q1_ref[...]
    q2 = q2_ref[...]

    def body(i, carry):
        k = k_ref[pl.ds(pl.multiple_of(i * tk, tk), tk), :]
        return (jnp.maximum(carry[0], jnp.max(_dot_nt(k, q1), axis=0, keepdims=True)),
                jnp.maximum(carry[1], jnp.max(_dot_nt(k, q2), axis=0, keepdims=True)))

    start = jnp.full((1, q1.shape[0]), NEG_BIG, F32)
    m1, m2 = lax.fori_loop(0, n_blocks, body, (start, start))
    m1_ref[...] = m1
    m2_ref[...] = m2


def score_rowmax(q1, q2, k, tq, tk):
    b, h, nq, _ = q1.shape
    nk = k.shape[2]
    qs = pl.BlockSpec((None, None, tq, V7X_LANES), lambda i, j, s: (i, j, s, 0))
    ms = pl.BlockSpec((None, None, 1, tq), lambda i, j, s: (i, j, 0, s))
    sd = jax.ShapeDtypeStruct((b, h, 1, nq), F32)
    return pl.pallas_call(
        functools.partial(_rowmax_body, n_blocks=nk // tk, tk=tk),
        grid=(b, h, nq // tq),
        in_specs=[qs, qs, pl.BlockSpec((None, None, nk, V7X_LANES), lambda i, j, s: (i, j, 0, 0))],
        out_specs=[ms, ms],
        out_shape=[sd, sd],
        compiler_params=_cparams(("parallel", "parallel", "parallel"), 32),
        name="score_rowmax",
    )(q1, q2, k)


def score_bound(qg, kg):
    return 1.01 * DA_HEAD_DIM ** 0.5 * jnp.max(jnp.abs(qg)) * jnp.max(jnp.abs(kg))


def stable_attention(bound, lam, q1, q2, sources, gcol, tq, bps, tk):
    zero = jnp.zeros(q1.shape[:2] + (1, q1.shape[2]), F32)

    def plain():
        return diff_attention(lam, q1, q2, zero, zero, sources, gcol, tq, bps, False)

    def shifted():
        keys = jnp.concatenate([k for k, _ in sources], axis=2)
        c1, c2 = score_rowmax(q1, q2, keys, tq, tk)
        return diff_attention(lam, q1, q2, c1, c2, sources, gcol, tq, bps, True)

    return lax.cond(2.0 * bound <= F32_EXP_RANGE, plain, shifted)


def _ret_body(qf_ref, kf_ref, vf_ref, gf_ref, cosf_ref, sinf_ref,
              qb_ref, kb_ref, vb_ref, gb_ref, cosb_ref, sinb_ref,
              dec_ref, intra_ref, seg_ref, s0_ref,
              yf_ref, yb_ref, sout_ref, state_ref, *, n_chunks, lc):
    step = pl.program_id(2)
    @pl.when(step == 0)
    def _():
        state_ref[...] = s0_ref[...]

    lane = lax.broadcasted_iota(jnp.int32, (1, V7X_LANES), 1)
    lower32 = (lane % RET_QK_DIM) < (RET_QK_DIM // 2)
    head0 = lane < RET_QK_DIM
    blk = (lax.broadcasted_iota(jnp.int32, (V7X_LANES, V7X_LANES), 0) // RET_QK_DIM ==
           lax.broadcasted_iota(jnp.int32, (V7X_LANES, V7X_LANES), 1) // RET_QK_DIM)
    seg = seg_ref[...]

    def rot(x, cos, sin):
        partner = jnp.where(lower32, pltpu.roll(x, 96, 1), pltpu.roll(x, 32, 1))
        return x * cos + partner * sin

    fwd_refs = (qf_ref, kf_ref, vf_ref, gf_ref, cosf_ref, sinf_ref, yf_ref)
    bwd_refs = (qb_ref, kb_ref, vb_ref, gb_ref, cosb_ref, sinb_ref, yb_ref)
    work = ([(0, c, fwd_refs) for c in range(n_chunks)] + [(1, c, bwd_refs) for c in reversed(range(n_chunks))])
    heads = (head0, jnp.logical_not(head0))

    def scores(d, c, refs):
        q_ref, k_ref, v_ref, _, cos_ref, sin_ref, _ = refs
        rows = pl.ds(c * lc, lc)
        cos = cos_ref[rows, :]
        sin = sin_ref[rows, :]
        q = rot(q_ref[rows, :], cos, sin)
        k = rot(k_ref[rows, :], cos, sin) * (RET_QK_DIM ** -0.5)
        v = v_ref[rows, :]
        kb = k.astype(BF16)
        sc = [_dot_nt(jnp.where(sel, q, 0.0).astype(BF16), kb) for sel in heads]
        kv = _dot_tn((k * dec_ref[d, 1]).astype(BF16), v.astype(BF16))
        return q.astype(BF16), v, sc, jnp.where(blk, kv, 0.0)

    stage1 = [scores(*w) for w in work]

    states = []
    for d in range(2):
        s = state_ref[d]
        cdec = dec_ref[d, 2][0:1, :]
        for (wd, _, _), (_, _, _, kv) in zip(work, stage1):
            if wd == d:
                states.append(s)
                s = s * cdec + kv
        state_ref[d] = s

    outs = []
    for (d, _, _), (qb, v, sc, _), s in zip(work, stage1, states):
        o = _dot(qb, s.astype(BF16)) * dec_ref[d, 0]
        for hh, sel in enumerate(heads):
            o = o + _dot((sc[hh] * intra_ref[d, hh]).astype(BF16), jnp.where(sel, v, 0.0).astype(BF16))
        outs.append(o)
    cens = [o - _seg_dot(o, seg) * (1.0 / RET_QK_DIM) for o in outs]
    variances = [_seg_dot(cen * cen, seg) * (1.0 / RET_QK_DIM) for cen in cens]
    for (_, c, refs), cen, var in zip(work, cens, variances):
        g = refs[3][pl.ds(c * lc, lc), :]
        refs[6][pl.ds(c * lc, lc), :] = cen * lax.rsqrt(var + NORM_EPS) * (g * _sigmoid(g))

    @pl.when(step == pl.num_programs(2) - 1)
    def _():
        sout_ref[...] = state_ref[...]


def retention(z, cos, sin, dec, intra, seg, s0, tm):
    b, r, _ = z.shape
    nb = r // tm
    lc = dec.shape[3]
    fwd = lambda off: pl.BlockSpec((None, tm, V7X_LANES), lambda i, p, k: (i, k, off + p))
    bwd = lambda off: pl.BlockSpec((None, tm, V7X_LANES), lambda i, p, k: (i, nb - 1 - k, off + p))
    tabf = pl.BlockSpec((tm, V7X_LANES), lambda i, p, k: (k, 0))
    tabb = pl.BlockSpec((tm, V7X_LANES), lambda i, p, k: (nb - 1 - k, 0))
    st = pl.BlockSpec((None, None, 2, V7X_LANES, V7X_LANES), lambda i, p, k: (i, p, 0, 0, 0))
    ysd = jax.ShapeDtypeStruct((b, r, 2 * V7X_LANES), F32)
    return pl.pallas_call(
        functools.partial(_ret_body, n_chunks=tm // lc, lc=lc),
        grid=(b, 2, nb),
        in_specs=[fwd(COL_RQ), fwd(COL_RK), fwd(COL_RV), fwd(COL_RGF), tabf, tabf,
                  bwd(COL_RQ), bwd(COL_RK), bwd(COL_RV), bwd(COL_RGB), tabb, tabb,
                  pl.BlockSpec((None, 2, 3, lc, V7X_LANES), lambda i, p, k: (p, 0, 0, 0, 0)),
                  pl.BlockSpec((None, 2, 2, lc, lc), lambda i, p, k: (p, 0, 0, 0, 0)),
                  pl.BlockSpec((V7X_LANES, V7X_LANES), lambda i, p, k: (0, 0)),
                  st],
        out_specs=[pl.BlockSpec((None, tm, V7X_LANES), lambda i, p, k: (i, k, p)),
                   pl.BlockSpec((None, tm, V7X_LANES), lambda i, p, k: (i, nb - 1 - k, p)),
                   st],
        out_shape=[ysd, ysd, jax.ShapeDtypeStruct(s0.shape, F32)],
        scratch_shapes=[pltpu.VMEM((2, V7X_LANES, V7X_LANES), F32)],
        compiler_params=_cparams(("parallel", "parallel", "arbitrary"), 32),
        name="retention",
    )(z, z, z, z, cos, sin, z, z, z, z, cos, sin, dec, intra, seg, s0)


def _sg_body(su_ref, sv_ref, ng_ref, w_ref, bm_ref, o_ref):
    tm = su_ref.shape[0]
    u = jax.nn.gelu(su_ref[...])
    t = jax.nn.gelu(sv_ref[...])
    mu = jnp.mean(t, axis=-1, keepdims=True)
    cen = t - mu
    var = jnp.mean(cen * cen, axis=-1, keepdims=True)
    v = (cen * lax.rsqrt(var + NORM_EPS) * ng_ref[...]).astype(BF16)
    group = lax.broadcasted_iota(jnp.int32, (1, 2 * V7X_LANES), 1) // SG_GROUP_DIM
    bias = bm_ref[...]
    for c in range(tm // SG_CHUNK):
        vc = v[c * SG_CHUNK:(c + 1) * SG_CHUNK, :]
        s = bias
        for g in range(SG_GROUPS):
            s = s + jnp.where(group == g, _dot(w_ref[g], vc), 0.0)
        o_ref[c * SG_CHUNK:(c + 1) * SG_CHUNK, :] = (u[c * SG_CHUNK:(c + 1) * SG_CHUNK, :] * s).astype(BF16)


def spatial_gating(z, ng, w, bm, tm):
    b, r, _ = z.shape
    wide = 2 * V7X_LANES
    return pl.pallas_call(
        _sg_body,
        grid=(b, r // tm),
        in_specs=[pl.BlockSpec((None, tm, wide), lambda i, j: (i, j, COL_SU256)),
                  pl.BlockSpec((None, tm, wide), lambda i, j: (i, j, COL_SV256)),
                  pl.BlockSpec((1, wide), lambda i, j: (0, 0)),
                  pl.BlockSpec((SG_GROUPS, SG_CHUNK, SG_CHUNK), lambda i, j: (0, 0, 0)),
                  pl.BlockSpec((SG_CHUNK, wide), lambda i, j: (0, 0))],
        out_specs=pl.BlockSpec((None, tm, wide), lambda i, j: (i, j, 0)),
        out_shape=jax.ShapeDtypeStruct((b, r, wide), BF16),
        compiler_params=_cparams(("parallel", "parallel"), 32),
        name="spatial_gating",
    )(z, z, ng, w, bm)


def _out_proj_body(a_ref, yf_ref, yb_ref, sg_ref, x_ref, g1_ref, a2_ref, s2_ref, w_ref, rwt_ref,
                   xo_ref, f_ref, aff_ref):
    na = a_ref.shape[1]
    nr = yf_ref.shape[1]
    tm = x_ref.shape[0]
    part = min(tm, 256)
    parts = [slice(s, s + part) for s in range(0, tm, part)]
    ys = []
    for rows in parts:
        r = (yf_ref[rows, :] + yb_ref[rows, :]).astype(BF16)
        ys.append(_dot(a_ref[rows, :], w_ref[0:na, :]) + _dot(r, w_ref[na:na + nr, :])
                  + _dot(sg_ref[rows, :], w_ref[na + nr:, :]))
    for rows, y in zip(parts, ys):
        xn = x_ref[rows, :] + g1_ref[...] * y
        xo_ref[rows, :] = xn
        ms = jnp.mean(xn * xn, axis=-1, keepdims=True)
        f = (xn * lax.rsqrt(ms + NORM_EPS) * a2_ref[...] + s2_ref[...]).astype(BF16)
        f_ref[rows, :] = f
        logits = _dot_nt(rwt_ref[...], f)
        e = jnp.exp(logits - jnp.max(logits, axis=0, keepdims=True))
        aff_ref[:, rows] = e / jnp.sum(e, axis=0, keepdims=True)


def out_proj(a, yf, yb, sg, x, g1, a2, s2, w, rwt, tm):
    b, r, d = x.shape
    rows = lambda width: pl.BlockSpec((None, tm, width), lambda i, j: (i, j, 0))
    vec = pl.BlockSpec((None, 1, d), lambda i, j: (i, 0, 0))
    return pl.pallas_call(
        _out_proj_body,
        grid=(b, r // tm),
        in_specs=[rows(a.shape[2]), rows(yf.shape[2]), rows(yb.shape[2]), rows(sg.shape[2]), rows(d),
                  vec, vec, vec,
                  pl.BlockSpec(w.shape, lambda i, j: (0, 0)),
                  pl.BlockSpec(rwt.shape, lambda i, j: (0, 0))],
        out_specs=[rows(d), rows(d), pl.BlockSpec((None, N_EXPERTS, tm), lambda i, j: (i, 0, j))],
        out_shape=[jax.ShapeDtypeStruct((b, r, d), F32), jax.ShapeDtypeStruct((b, r, d), BF16),
                   jax.ShapeDtypeStruct((b, N_EXPERTS, r), F32)],
        compiler_params=_cparams(("parallel", "parallel"), 48),
        name="out_proj",
    )(a, yf, yb, sg, x, g1, a2, s2, w, rwt)


def _router_body(aff_ref, tri_ref, posm_ref, off_ref, *, cap, tb):
    aff = aff_ref[...]
    e, r = aff.shape
    nb = r // tb
    bits = pltpu.bitcast(aff, jnp.int32)

    def count_at_least(x, t):
        return jnp.sum(jnp.where(x >= t, 1.0, 0.0), axis=1, keepdims=True)

    def coarse(i, ebits):
        cand = ebits | lax.shift_left(jnp.int32(1), 30 - i)
        return jnp.where(count_at_least(bits, cand) >= cap, cand, ebits)

    base = pltpu.bitcast(lax.fori_loop(0, 8, coarse, jnp.zeros((e, 1), jnp.int32)), F32)

    def fine(i, bracket):
        lo, hi = bracket
        mid = 0.5 * (lo + hi)
        keep = count_at_least(aff, mid) >= cap
        return jnp.where(keep, mid, lo), jnp.where(keep, hi, mid)

    lo, hi = lax.fori_loop(0, 32, fine, (0.5 * base, jnp.maximum(4.0 * base, 2.0 ** -124)))
    gt = jnp.where(aff >= hi, 1.0, 0.0)
    eq = jnp.where(aff >= lo, 1.0, 0.0) - gt
    need = cap - jnp.sum(gt, axis=1, keepdims=True)
    tri = tri_ref[...]
    lane = lax.broadcasted_iota(jnp.int32, (e, V7X_LANES), 1)
    offs = jnp.zeros((e, V7X_LANES), jnp.int32)
    run_eq = jnp.zeros((e, 1), F32)
    run_sel = jnp.zeros((e, 1), F32)
    for j in range(nb):
        cols = slice(j * tb, (j + 1) * tb)
        eq_j = eq[:, cols]
        before_eq = _dot(eq_j.astype(BF16), tri) + run_eq
        sel_j = gt[:, cols] + eq_j * jnp.where(before_eq < need, 1.0, 0.0)
        pos_j = _dot(sel_j.astype(BF16), tri) + run_sel
        posm_ref[:, cols] = jnp.where(sel_j > 0.0, pos_j, -1.0)
        offs = jnp.where(lane == j, run_sel.astype(jnp.int32), offs)
        run_eq = run_eq + jnp.sum(eq_j, axis=1, keepdims=True)
        run_sel = run_sel + jnp.sum(sel_j, axis=1, keepdims=True)
    off_ref[...] = jnp.where(lane == nb, run_sel.astype(jnp.int32), offs)


def route(aff_t, tri, cap):
    b, e, r = aff_t.shape
    tb = tri.shape[0]
    posm, off = pl.pallas_call(
        functools.partial(_router_body, cap=cap, tb=tb),
        grid=(b,),
        in_specs=[pl.BlockSpec((None, e, r), lambda i: (i, 0, 0)),
                  pl.BlockSpec((tb, tb), lambda i: (0, 0))],
        out_specs=[pl.BlockSpec((None, e, r), lambda i: (i, 0, 0)),
                   pl.BlockSpec((None, e, V7X_LANES), lambda i: (i, 0, 0))],
        out_shape=[jax.ShapeDtypeStruct((b, e, r), F32), jax.ShapeDtypeStruct((b, e, V7X_LANES), jnp.int32)],
        compiler_params=_cparams(("parallel",), 32),
        name="router",
    )(aff_t, tri)
    shape5 = (b, e, r // tb, 1, tb)
    return posm.reshape(shape5), aff_t.reshape(shape5), off.reshape(-1)


def _gather_body(off_ref, f_ref, posm_ref, aff_ref, x_hbm, g_hbm, xall, gall, pcat, sem, *, cap, w, nb, n_exp):
    b = pl.program_id(0)
    j = pl.program_id(1)
    align = V7X_BF16_SUBLANES

    @pl.when(j == 0)
    def _():
        xall[...] = jnp.zeros(xall.shape, BF16)
        gall[...] = jnp.zeros(gall.shape, F32)

    iota = lax.broadcasted_iota(jnp.int32, (w, 1), 0).astype(F32)

    def window(e, rnd):
        off = off_ref[(b * n_exp + e) * V7X_LANES + j]
        return pl.multiple_of(jnp.minimum((off // align) * align + rnd * w, cap), align)

    def one_round(rnd):
        for e in range(n_exp):
            first = window(e, rnd)
            onehot = posm_ref[e] == first.astype(F32) + iota
            pcat[e * w:(e + 1) * w, :] = jnp.where(onehot, 1.0, 0.0).astype(BF16)
            gall[e, pl.ds(first, w), :] += jnp.sum(jnp.where(onehot, aff_ref[e], 0.0), axis=1, keepdims=True)
        rows = _dot(pcat[...], f_ref[...])
        for e in range(n_exp):
            xall[e, pl.ds(window(e, rnd), w), :] += rows[e * w:(e + 1) * w, :].astype(BF16)

    one_round(0)
    span = jnp.int32(0)
    for e in range(n_exp):
        idx = (b * n_exp + e) * V7X_LANES + j
        span = jnp.maximum(span, off_ref[idx + 1] - (off_ref[idx] // align) * align)

    def extra(rnd, carry):
        one_round(rnd)
        return carry

    lax.fori_loop(1, (span + w - 1) // w, extra, 0)

    @pl.when(j == nb - 1)
    def _():
        cx = pltpu.make_async_copy(xall.at[:, pl.ds(0, cap), :], x_hbm.at[b], sem.at[0])
        cg = pltpu.make_async_copy(gall.at[:, pl.ds(0, cap), :], g_hbm.at[b], sem.at[1])
        cx.start()
        cg.start()
        cx.wait()
        cg.wait()


def moe_gather(off, f, posm, aff, cap, w):
    b, r, d = f.shape
    _, e, nb, _, tb = posm.shape
    blk = pl.BlockSpec((None, e, None, 1, tb), lambda i, j, o: (i, 0, j, 0, 0))
    return pl.pallas_call(
        functools.partial(_gather_body, cap=cap, w=w, nb=nb, n_exp=e),
        grid_spec=pltpu.PrefetchScalarGridSpec(
            num_scalar_prefetch=1,
            grid=(b, nb),
            in_specs=[pl.BlockSpec((None, tb, d), lambda i, j, o: (i, j, 0)), blk, blk],
            out_specs=[pl.BlockSpec(memory_space=pl.ANY), pl.BlockSpec(memory_space=pl.ANY)],
            scratch_shapes=[pltpu.VMEM((e, cap + w, d), BF16), pltpu.VMEM((e, cap + w, 1), F32),
                            pltpu.VMEM((e * w, tb), BF16), pltpu.SemaphoreType.DMA((2,))]),
        out_shape=[jax.ShapeDtypeStruct((b, e, cap, d), BF16), jax.ShapeDtypeStruct((b, e, cap, 1), F32)],
        compiler_params=_cparams(("arbitrary", "arbitrary"), 56),
        name="moe_gather",
    )(off, f, posm, aff)


def _moe_ffn_body(*refs, has_ctx, cap, capc):
    if has_ctx:
        _, x_ref, g_ref, xc_ref, gc_ref, wg_ref, wu_ref, wd_ref, y_ref, yc_ref, yacc, ycacc = refs
    else:
        _, x_ref, g_ref, wg_ref, wu_ref, wd_ref, y_ref, yacc = refs
    c = pl.program_id(1)
    last = pl.num_programs(1) - 1
    nbatch, _, d = x_ref.shape
    wg = wg_ref[...].astype(BF16)
    wu = wu_ref[...].astype(BF16)
    wd = wd_ref[...].astype(BF16)

    def hidden(x):
        a = _dot(x, wg)
        u = _dot(x, wu)
        return (a * _sigmoid(a) * u).astype(BF16)

    @pl.when(c == 0)
    def _():
        yacc[...] = jnp.zeros(yacc.shape, F32)
        if has_ctx:
            ycacc[...] = jnp.zeros(ycacc.shape, F32)

    xs = [x_ref[i] for i in range(nbatch)]
    if has_ctx:
        xs[-1] = jnp.concatenate([xs[-1], xc_ref[...].reshape(nbatch * capc, d)], axis=0)
    hs = [hidden(x) for x in xs]
    ys = [_dot(h, wd) for h in hs]
    for i in range(nbatch):
        yacc[i] += ys[i][0:cap]
    if has_ctx:
        ycacc[...] += ys[-1][cap:]

    @pl.when(c == last)
    def _():
        for i in range(nbatch):
            y_ref[i, 0:cap, :] = (yacc[i] * g_ref[i]).astype(BF16)
            y_ref[i, cap:, :] = jnp.zeros((y_ref.shape[1] - cap, d), BF16)
        if has_ctx:
            yc = ycacc[...] * gc_ref[...].reshape(nbatch * capc, 1)
            yc_ref[:, 0:capc, :] = yc.reshape(nbatch, capc, d).astype(BF16)
            yc_ref[:, capc:, :] = jnp.zeros((nbatch, yc_ref.shape[1] - capc, d), BF16)


def moe_ffn(layer, x, g, xc, gc, wg, wu, wd, tf, pad, padc):
    b, e, cap, d = x.shape
    ff = wg.shape[3]
    has_ctx = xc is not None
    capc = xc.shape[2] if has_ctx else 0
    tok = lambda rows, width: pl.BlockSpec((b, None, rows, width), lambda j, k, l: (0, j, 0, 0))
    in_specs = [tok(cap, d), tok(cap, 1)]
    out_specs = [tok(cap + pad, d)]
    out_shape = [jax.ShapeDtypeStruct((b, e, cap + pad, d), BF16)]
    scratch = [pltpu.VMEM((b, cap, d), F32)]
    args = [x, g]
    if has_ctx:
        in_specs += [tok(capc, d), tok(capc, 1)]
        out_specs.append(tok(capc + padc, d))
        out_shape.append(jax.ShapeDtypeStruct((b, e, capc + padc, d), BF16))
        scratch.append(pltpu.VMEM((b * capc, d), F32))
        args += [xc, gc]
    in_specs += [pl.BlockSpec((None, None, d, tf), lambda j, k, l: (l[0], j, 0, k)),
                 pl.BlockSpec((None, None, d, tf), lambda j, k, l: (l[0], j, 0, k)),
                 pl.BlockSpec((None, None, tf, d), lambda j, k, l: (l[0], j, k, 0))]
    out = pl.pallas_call(
        functools.partial(_moe_ffn_body, has_ctx=has_ctx, cap=cap, capc=capc),
        grid_spec=pltpu.PrefetchScalarGridSpec(
            num_scalar_prefetch=1, grid=(e, ff // tf),
            in_specs=in_specs, out_specs=out_specs, scratch_shapes=scratch),
        out_shape=out_shape,
        compiler_params=_cparams(("parallel", "arbitrary"), 56),
        name="moe_ffn",
    )(layer, *args, wg, wu, wd)
    return out if has_ctx else (out[0], None)


def _scatter_body(off_ref, x_ref, g2_ref, posm_ref, y_hbm, xo_ref, ybuf, pcat, sem, *, cap, w, nb, n_exp):
    b = pl.program_id(0)
    j = pl.program_id(1)
    step = b * nb + j
    slot = lax.rem(step, 2)
    spare = 2
    align = V7X_BF16_SUBLANES

    def window(bb, jj, e, rnd):
        off = off_ref[(bb * n_exp + e) * V7X_LANES + jj]
        return jnp.minimum((off // align) * align + rnd * w, cap)

    def copies(bb, jj, buf, rnd):
        return [pltpu.make_async_copy(
            y_hbm.at[bb, e, pl.ds(pl.multiple_of(window(bb, jj, e, rnd), align), w), :],
            ybuf.at[buf, pl.ds(e * w, w), :], sem.at[buf]) for e in range(n_exp)]

    @pl.when(step == 0)
    def _():
        for cp in copies(b, j, 0, 0):
            cp.start()

    @pl.when(step + 1 < pl.num_programs(0) * nb)
    def _():
        nxt = step + 1
        for cp in copies(nxt // nb, lax.rem(nxt, nb), 1 - slot, 0):
            cp.start()

    iota = lax.broadcasted_iota(jnp.int32, (w, 1), 0).astype(F32)

    def onehots(rnd):
        for e in range(n_exp):
            first = window(b, j, e, rnd).astype(F32)
            pcat[e * w:(e + 1) * w, :] = jnp.where(posm_ref[e] == first + iota, 1.0, 0.0).astype(BF16)

    for cp in copies(b, j, slot, 0):
        cp.wait()
    onehots(0)
    g2 = g2_ref[...]
    xo_ref[...] = x_ref[...] + g2 * _dot_tn(pcat[...], ybuf[slot])

    span = jnp.int32(0)
    for e in range(n_exp):
        idx = (b * n_exp + e) * V7X_LANES + j
        span = jnp.maximum(span, off_ref[idx + 1] - (off_ref[idx] // align) * align)

    def extra(rnd, carry):
        for cp in copies(b, j, spare, rnd):
            cp.start()
        for cp in copies(b, j, spare, rnd):
            cp.wait()
        onehots(rnd)
        xo_ref[...] += g2 * _dot_tn(pcat[...], ybuf[spare])
        return carry

    lax.fori_loop(1, (span + w - 1) // w, extra, 0)


def moe_scatter(off, x, g2, posm, y, cap, w):
    b, r, d = x.shape
    _, e, nb, _, tb = posm.shape
    rows = pl.BlockSpec((None, tb, d), lambda i, j, o: (i, j, 0))
    return pl.pallas_call(
        functools.partial(_scatter_body, cap=cap, w=w, nb=nb, n_exp=e),
        grid_spec=pltpu.PrefetchScalarGridSpec(
            num_scalar_prefetch=1,
            grid=(b, nb),
            in_specs=[rows, pl.BlockSpec((None, 1, d), lambda i, j, o: (i, 0, 0)),
                      pl.BlockSpec((None, e, None, 1, tb), lambda i, j, o: (i, 0, j, 0, 0)),
                      pl.BlockSpec(memory_space=pl.ANY)],
            out_specs=rows,
            scratch_shapes=[pltpu.VMEM((3, e * w, d), BF16), pltpu.VMEM((e * w, tb), BF16),
                            pltpu.SemaphoreType.DMA((3,))]),
        out_shape=jax.ShapeDtypeStruct((b, r, d), F32),
        compiler_params=_cparams(("arbitrary", "arbitrary"), 32),
        name="moe_scatter",
    )(off, x, g2, posm, y)


def _attn_tables(n):
    rows = n // GRID_W
    row = jnp.broadcast_to(jnp.arange(rows, dtype=F32)[:, None], (rows, GRID_W)).reshape(-1)
    col = jnp.broadcast_to(jnp.arange(GRID_W, dtype=F32)[None, :], (rows, GRID_W)).reshape(-1)
    quarter = DA_HEAD_DIM // 4
    inv = ROPE_BASE ** (-jnp.arange(quarter, dtype=F32) / quarter)
    ang_r, ang_c = row[:, None] * inv, col[:, None] * inv

    def half(ang):
        return (jnp.concatenate([jnp.cos(ang), jnp.cos(ang)], -1),
                jnp.concatenate([-jnp.sin(ang), jnp.sin(ang)], -1))

    cr, sr = half(ang_r)
    cc, sc = half(ang_c)
    cos64 = jnp.concatenate([cr, cc], -1)
    sin64 = jnp.concatenate([sr, sc], -1)
    return jnp.tile(cos64, (1, 2)), jnp.tile(sin64, (1, 2))


def _ret_tables(n):
    inv = 1.0 / (ROPE_BASE ** jnp.linspace(0.0, 1.0, RET_QK_DIM // 2, dtype=F32))
    ang = jnp.arange(n, dtype=F32)[:, None] * inv
    cos64 = jnp.concatenate([jnp.cos(ang), jnp.cos(ang)], -1)
    sin64 = jnp.concatenate([-jnp.sin(ang), jnp.sin(ang)], -1)
    return jnp.tile(cos64, (1, 2)), jnp.tile(sin64, (1, 2))


def _ret_decay_tables(log_gamma, lc):
    pos = jnp.arange(lc, dtype=F32)
    lg = jnp.repeat(log_gamma, RET_QK_DIM, axis=1).reshape(2, 2, 1, V7X_LANES)
    qd = jnp.stack([jnp.exp(lg[0] * (pos + 1.0)[None, :, None]), jnp.exp(lg[1] * (lc - pos)[None, :, None])])
    kd = jnp.stack([jnp.exp(lg[0] * (lc - 1.0 - pos)[None, :, None]), jnp.exp(lg[1] * pos[None, :, None])])
    cd = jnp.broadcast_to(jnp.exp(lg * lc), (2, 2, lc, V7X_LANES))
    dec = jnp.stack([qd, kd, cd], axis=2)
    dist = pos[:, None] - pos[None, :]
    lgh = log_gamma[:, :, None, None]
    fwd = jnp.where(dist >= 0, jnp.exp(lgh[0] * jnp.maximum(dist, 0.0)), 0.0)
    bwd = jnp.where(dist <= 0, jnp.exp(lgh[1] * jnp.maximum(-dist, 0.0)), 0.0)
    intra = jnp.stack([fwd, bwd]).reshape(2, 2, 2, lc, lc)
    return jnp.swapaxes(dec, 0, 1), jnp.swapaxes(intra, 0, 1)


def kernel(x, c, ctx, c_ctx, w_mod, b_mod, norm1_g, norm2_g, w_in, w_out, da_q_norm_g, da_k_norm_g,
           da_lambda_q1, da_lambda_k1, da_lambda_q2, da_lambda_k2, da_subln_g, ret_decay,
           sg_norm_g, sg_w, sg_b, router_w, ex_w_gate, ex_w_up, ex_w_down):
    b, n, d = x.shape
    n_ctx = ctx.shape[1]
    cos_a, sin_a = _attn_tables(n)
    cos_r, sin_r = _ret_tables(n)
    one_c = jnp.ones((n_ctx, V7X_LANES), F32)
    zero_c = jnp.zeros((n_ctx, V7X_LANES), F32)
    lane = jnp.arange(V7X_LANES)
    seg = (lane[:, None] // 64 == lane[None, :] // 64).astype(BF16)
    state0 = jnp.zeros((b, 2, 2, V7X_LANES, V7X_LANES), F32)
    tk = n_ctx
    assert (n + n_ctx) // tk % ATTN_BLOCKS_PER_STEP == 0
    tok = jnp.arange(MOE_TOKEN_BLOCK)
    tri = (tok[:, None] < tok[None, :]).astype(BF16)
    cap = EC_CAPACITY * n // N_EXPERTS
    cap_c = EC_CAPACITY * n_ctx // N_EXPERTS
    pad, pad_c = min(64, cap), min(64, cap_c)

    mods = modulation(jnp.concatenate([c, c_ctx[None, :]], axis=0), w_mod, b_mod)

    for layer in range(DEPTH):
        need_ctx = layer < DEPTH - 1
        lambda_init = 0.8 - 0.6 * math.exp(-0.3 * layer)
        mod = mods[layer, 0:b]
        mod_c = jnp.broadcast_to(mods[layer, b], mod.shape)
        sh1, sc1, g1, sh2, sc2, g2 = jnp.split(mod[:, None, :], 6, axis=-1)
        csh1, csc1, cg1, csh2, csc2, cg2 = jnp.split(mod_c[:, None, :], 6, axis=-1)
        lam = (jnp.exp(jnp.sum(da_lambda_q1[layer] * da_lambda_k1[layer]))
               - jnp.exp(jnp.sum(da_lambda_q2[layer] * da_lambda_k2[layer])) + lambda_init).reshape(1)
        log_gamma = jax.nn.log_sigmoid(ret_decay[layer].astype(F32))
        dec, intra = _ret_decay_tables(log_gamma, RET_SCAN_CHUNK)
        w_in_b = w_in[layer].astype(BF16)
        w_out_b = w_out[layer].astype(BF16)
        rwt = router_w[layer].T.astype(BF16)
        qg = jnp.tile(da_q_norm_g[layer], 2)[None, :]
        kg = jnp.tile(da_k_norm_g[layer], 2)[None, :]
        gcol = (da_subln_g[layer] * (1.0 - lambda_init))[:, None]
        sg_ng = sg_norm_g[layer][None, :]
        sg_wb = sg_w[layer].astype(BF16)
        sg_bm = jnp.repeat(sg_b[layer].T, SG_GROUP_DIM, axis=1)
        n1, n2 = norm1_g[layer][None, None, :], norm2_g[layer][None, None, :]

        z_lat = in_proj(x, n1 * (1.0 + sc1), sh1, w_in_b, 512)
        z_ctx = in_proj(ctx, n1 * (1.0 + csc1), csh1, w_in_b, n_ctx)

        cq1, cq2, ck, cvt = qkv_prep(z_ctx, one_c, zero_c, qg, kg, seg, n_ctx, tk)
        q1, q2, kl, vlt = qkv_prep(z_lat, cos_a, sin_a, qg, kg, seg, _row_tile(n, 1024), tk)
        sources = [(ck, cvt), (kl, vlt)]
        bound = score_bound(da_q_norm_g[layer], da_k_norm_g[layer])
        a_lat = stable_attention(bound, lam, q1, q2, sources, gcol, ATTN_QUERY_TILE, ATTN_BLOCKS_PER_STEP, tk)

        cyf, cyb, cstate = retention(z_ctx, one_c, zero_c, dec, intra, seg, state0, n_ctx)
        yf, yb, _ = retention(z_lat, cos_r, sin_r, dec, intra, seg, cstate, _row_tile(n, 1024))

        s_lat = spatial_gating(z_lat, sg_ng, sg_wb, sg_bm, 512)

        x, f_lat, aff_lat = out_proj(a_lat, yf, yb, s_lat, x, g1, n2 * (1.0 + sc2), sh2, w_out_b, rwt,
                                     _row_tile(n, 1024))
        posm, aff5, off = route(aff_lat, tri, cap)
        xs, gs = moe_gather(off, f_lat, posm, aff5, cap, pad)
        xcs = gcs = None
        if need_ctx:
            a_ctx = stable_attention(bound, lam, cq1, cq2, sources[:1], gcol, n_ctx, 1, tk)
            s_ctx = spatial_gating(z_ctx, sg_ng, sg_wb, sg_bm, n_ctx)
            ctx, f_ctx, aff_ctx = out_proj(a_ctx, cyf, cyb, s_ctx, ctx, cg1, n2 * (1.0 + csc2), csh2,
                                           w_out_b, rwt, n_ctx)
            posm_c, aff5_c, off_c = route(aff_ctx, tri, cap_c)
            xcs, gcs = moe_gather(off_c, f_ctx, posm_c, aff5_c, cap_c, pad_c)
        ys, ycs = moe_ffn(jnp.full((1,), layer, jnp.int32), xs, gs, xcs, gcs, ex_w_gate, ex_w_up, ex_w_down,
                          512, pad, pad_c)
        x = moe_scatter(off, x, g2, posm, ys, cap, pad)
        if need_ctx:
            ctx = moe_scatter(off_c, ctx, cg2, posm_c, ycs, cap_c, pad_c)
    return x
```
